```python
import math
import jax, jax.numpy as jnp
from jax import lax
import numpy as np

D_MODEL = 1024
BATCH = 32
SEQ = 2048
DEPTH = 4

CTX_LEN = 256
GRID_W = 64
N_MIXERS = 3
DA_HEADS = 8
DA_HEAD_DIM = 64
DA_V_DIM = 2 * DA_HEAD_DIM
ROPE_F = DA_HEAD_DIM // 4
ROPE_THETA = 10000.0
Q_BLOCK = 128
FN_GROUPS = 8
FN_GROUP_DIM = D_MODEL // FN_GROUPS
HY_ORDER = 2
HY_EMB = 33
HY_BANDS = (HY_EMB - 1) // 2
HY_FILTER_HIDDEN = 64
HY_SIN_FREQ = 1.0
HY_FAST_DECAY = 0.3
HY_SLOW_DECAY = 1.5
HY_TARGET = 1e-2
HY_MAX_DECAY = math.log(HY_TARGET) / HY_FAST_DECAY
HY_MIN_DECAY = math.log(HY_TARGET) / HY_SLOW_DECAY
D_FF = 3584
N_EXPERTS = 8
TOP_K = 2
MOE_BLOCK = 128
EPS = 1e-6

kernel_name = 'hybrid_dit_diffattn_fnet_hyena_moe'


def rms_norm(x, g):
    xf = x.astype(jnp.float32)
    y = xf * lax.rsqrt(jnp.mean(xf * xf, axis=-1, keepdims=True) + EPS)
    return (y * g.astype(jnp.float32)).astype(x.dtype)


def axial_rope(x, cos, sin):
    xr = x.reshape(x.shape[:-1] + (2, 2, ROPE_F))
    cs = cos[None, :, None, None].astype(x.dtype)
    sn = sin[None, :, None, None].astype(x.dtype)
    x1, x2 = xr[..., 0, :], xr[..., 1, :]
    out = jnp.stack([x1 * cs - x2 * sn, x2 * cs + x1 * sn], axis=-2)
    return out.reshape(x.shape)


def diff_attention(h_l, h_c, cos, sin, w_qkv, lq1, lk1, lq2, lk2, subln_g, w_o, lam_init, with_ctx):
    B, L, D = h_l.shape

    def project(h):
        n = h.shape[1]
        q, k, v = jnp.split(h @ w_qkv, 3, axis=-1)
        return (q.reshape(B, n, DA_HEADS, 2, DA_HEAD_DIM),
                k.reshape(B, n, DA_HEADS, 2, DA_HEAD_DIM),
                v.reshape(B, n, DA_HEADS, DA_V_DIM))

    q_l, k_l, v_l = project(h_l)
    q_c, k_c, v_c = project(h_c)
    q_l = axial_rope(q_l, cos, sin)
    k_l = axial_rope(k_l, cos, sin)
    f32 = jnp.float32
    lam = (jnp.exp(jnp.sum(lq1.astype(f32) * lk1.astype(f32)))
           - jnp.exp(jnp.sum(lq2.astype(f32) * lk2.astype(f32))) + lam_init)
    scale = DA_HEAD_DIM ** -0.5

    def attend(q, k, v):
        n = q.shape[1]
        s = jnp.einsum('bqhnd,bkhnd->bnhqk', q, k).astype(f32) * scale
        p = jax.nn.softmax(s, axis=-1)
        a = (p[:, 0] - lam * p[:, 1]).astype(v.dtype)
        o = jnp.einsum('bhqk,bkhe->bqhe', a, v)
        o = rms_norm(o, subln_g) * (1.0 - lam_init)
        return o.reshape(B, n, DA_HEADS * DA_V_DIM)

    k_all = jnp.concatenate([k_c, k_l], axis=1)
    v_all = jnp.concatenate([v_c, v_l], axis=1)
    qb = q_l.reshape(B, L // Q_BLOCK, Q_BLOCK, DA_HEADS, 2, DA_HEAD_DIM).swapaxes(0, 1)
    o_l = lax.map(lambda q: attend(q, k_all, v_all), qb)
    y_l = o_l.swapaxes(0, 1).reshape(B, L, -1) @ w_o
    y_c = attend(q_c, k_c, v_c) @ w_o if with_ctx else None
    return y_l, y_c


def fourier_mix(h, w_o, b_o):
    B, L, D = h.shape
    hg = h.astype(jnp.float32).reshape(B, L, FN_GROUPS, FN_GROUP_DIM)
    f = jnp.fft.fftn(hg, axes=(1, 3), norm='ortho').real
    return f.reshape(B, L, D).astype(h.dtype) @ w_o + b_o


def short_conv(u, w, b):
    up = jnp.pad(u, ((0, 0), (1, 1), (0, 0)))
    return up[:, :-2] * w[0] + up[:, 1:-1] * w[1] + up[:, 2:] * w[2] + b


def hyena_filters(L, f_w1, f_b1, f_w2, f_b2, f_w3):
    f32 = jnp.float32
    D = f_w3.shape[-1] // (2 * HY_ORDER)
    t01 = jnp.linspace(0.0, 1.0, L, dtype=f32)[:, None]
    w = 2.0 * math.pi * jnp.arange(L, dtype=f32) / L
    bands = jnp.linspace(1e-4, HY_BANDS - 1, HY_BANDS, dtype=f32)
    fw = w[:, None] * bands[None, :]
    emb = jnp.concatenate([t01, jnp.cos(fw), -jnp.sin(fw)], axis=-1)
    hdn = jnp.sin(HY_SIN_FREQ * (emb @ f_w1.astype(f32) + f_b1.astype(f32)))
    hdn = jnp.sin(HY_SIN_FREQ * (hdn @ f_w2.astype(f32) + f_b2.astype(f32)))
    filt = (hdn @ f_w3.astype(f32)).reshape(L, HY_ORDER, 2, D)
    deltas = jnp.abs(jnp.linspace(HY_MIN_DECAY, HY_MAX_DECAY, D, dtype=f32))
    decay = jnp.exp(-t01 * deltas[None, :])
    filt = filt * decay[:, None, None, :]
    fwd, bwd = filt[:, :, 0], filt[:, :, 1]
    two = jnp.concatenate([fwd, jnp.zeros((1, HY_ORDER, D), f32), bwd[1:][::-1]], axis=0)
    return jnp.fft.rfft(two, axis=0)


def hyena_mixer(h, w_in, conv_w, conv_b, f_w1, f_b1, f_w2, f_b2, f_w3, f_bias, w_o):
    L = h.shape[1]
    u = short_conv(h @ w_in, conv_w, conv_b)
    x1, x2, v = jnp.split(u, 3, axis=-1)
    kf = hyena_filters(L, f_w1, f_b1, f_w2, f_b2, f_w3)
    z = v
    for n, gate in enumerate((x1, x2)):
        zf = jnp.fft.rfft(z.astype(jnp.float32), n=2 * L, axis=1)
        y = jnp.fft.irfft(zf * kf[None, :, n], n=2 * L, axis=1)[:, :L]
        z = gate * (y.astype(h.dtype) + f_bias[n] * z)
    return z @ w_o


def swiglu(h, w1, w3, w2):
    return (jax.nn.silu(h @ w1) * (h @ w3)) @ w2


def moe_swiglu(h, router, w1, w3, w2):
    T, D = h.shape
    logits = (h @ router).astype(jnp.float32)
    top_v, top_i = lax.top_k(logits, TOP_K)
    gates = jax.nn.softmax(top_v, axis=-1).astype(h.dtype)
    flat_e = top_i.reshape(-1)
    flat_w = gates.reshape(-1)
    flat_tok = jnp.repeat(jnp.arange(T, dtype=jnp.int32), TOP_K)
    TK = T * TOP_K
    counts = jnp.bincount(flat_e, length=N_EXPERTS)
    padded = (counts + MOE_BLOCK - 1) // MOE_BLOCK * MOE_BLOCK
    pad_start = jnp.cumsum(padded) - padded
    start = jnp.cumsum(counts) - counts
    order = jnp.argsort(flat_e)
    se = flat_e[order]
    dest = pad_start[se] + jnp.arange(TK, dtype=jnp.int32) - start[se]
    n_blocks = -(-TK // MOE_BLOCK) + N_EXPERTS
    n_rows = n_blocks * MOE_BLOCK
    row_tok = jnp.full((n_rows,), T, jnp.int32).at[dest].set(flat_tok[order])
    row_w = jnp.zeros((n_rows,), h.dtype).at[dest].set(flat_w[order])
    block_start = jnp.arange(n_blocks, dtype=jnp.int32) * MOE_BLOCK
    block_e = jnp.minimum(jnp.searchsorted(jnp.cumsum(padded), block_start, side='right'), N_EXPERTS - 1)
    h_ext = jnp.concatenate([h, jnp.zeros((1, D), h.dtype)], axis=0)
    xb = h_ext[row_tok].reshape(n_blocks, MOE_BLOCK, D)

    def expert_block(args):
        xe, e = args
        return swiglu(xe, w1[e], w3[e], w2[e])

    yb = lax.map(expert_block, (xb, block_e))
    y = jnp.zeros((T + 1, D), h.dtype).at[row_tok].add(yb.reshape(n_rows, D) * row_w[:, None])
    return y[:T]


def setup_inputs(seed: int = 0) -> dict:
    key = jax.random.key(seed)
    ks = iter(jax.random.split(key, 64))
    D = D_MODEL

    def nrm(shape, s):
        return jax.random.normal(next(ks), shape, jnp.float32) * s

    n_a = len(range(0, DEPTH, N_MIXERS))
    n_b = len(range(1, DEPTH, N_MIXERS))
    n_c = len(range(2, DEPTH, N_MIXERS))
    n_dense = len(range(0, DEPTH, 2))
    n_moe = len(range(1, DEPTH, 2))
    return {
        'x': nrm((BATCH, SEQ, D), 1.0),
        'c': nrm((BATCH, D), 1.0),
        'ctx': nrm((BATCH, CTX_LEN, D), 1.0),
        'c_ctx': nrm((D,), 1.0),
        'ada_w': nrm((DEPTH, D, 6 * D), 0.5 * D ** -0.5),
        'ada_b': nrm((DEPTH, 6 * D), 0.02),
        'norm1_g': 1.0 + nrm((DEPTH, D), 0.02),
        'norm2_g': 1.0 + nrm((DEPTH, D), 0.02),
        'attn_w_qkv': nrm((n_a, D, 3 * D), D ** -0.5),
        'attn_lq1': nrm((n_a, DA_HEAD_DIM), 0.1),
        'attn_lk1': nrm((n_a, DA_HEAD_DIM), 0.1),
        'attn_lq2': nrm((n_a, DA_HEAD_DIM), 0.1),
        'attn_lk2': nrm((n_a, DA_HEAD_DIM), 0.1),
        'attn_subln_g': 1.0 + nrm((n_a, DA_V_DIM), 0.02),
        'attn_w_o': nrm((n_a, D, D), D ** -0.5),
        'fnet_w_o': nrm((n_b, D, D), D ** -0.5),
        'fnet_b_o': nrm((n_b, D), 0.02),
        'hy_w_in': nrm((n_c, D, 3 * D), D ** -0.5),
        'hy_conv_w': nrm((n_c, 3, 3 * D), 0.5),
        'hy_conv_b': nrm((n_c, 3 * D), 0.02),
        'hy_f_w1': nrm((n_c, HY_EMB, HY_FILTER_HIDDEN), HY_EMB ** -0.5),
        'hy_f_b1': nrm((n_c, HY_FILTER_HIDDEN), 0.1),
        'hy_f_w2': nrm((n_c, HY_FILTER_HIDDEN, HY_FILTER_HIDDEN), HY_FILTER_HIDDEN ** -0.5),
        'hy_f_b2': nrm((n_c, HY_FILTER_HIDDEN), 0.1),
        'hy_f_w3': nrm((n_c, HY_FILTER_HIDDEN, HY_ORDER * 2 * D), 0.2 * HY_FILTER_HIDDEN ** -0.5),
        'hy_f_bias': nrm((n_c, HY_ORDER, D), 0.5),
        'hy_w_o': nrm((n_c, D, D), D ** -0.5),
        'ff_w1': nrm((n_dense, D, D_FF), D ** -0.5),
        'ff_w3': nrm((n_dense, D, D_FF), D ** -0.5),
        'ff_w2': nrm((n_dense, D_FF, D), D_FF ** -0.5),
        'moe_router': nrm((n_moe, D, N_EXPERTS), D ** -0.5),
        'moe_w1': nrm((n_moe, N_EXPERTS, D, D_FF), D ** -0.5),
        'moe_w3': nrm((n_moe, N_EXPERTS, D, D_FF), D ** -0.5),
        'moe_w2': nrm((n_moe, N_EXPERTS, D_FF, D), D_FF ** -0.5),
        'final_g': 1.0 + nrm((D,), 0.02),
    }


def reference(x, c, ctx, c_ctx, ada_w, ada_b, norm1_g, norm2_g,
              attn_w_qkv, attn_lq1, attn_lk1, attn_lq2, attn_lk2, attn_subln_g, attn_w_o,
              fnet_w_o, fnet_b_o,
              hy_w_in, hy_conv_w, hy_conv_b, hy_f_w1, hy_f_b1, hy_f_w2, hy_f_b2, hy_f_w3, hy_f_bias, hy_w_o,
              ff_w1, ff_w3, ff_w2, moe_router, moe_w1, moe_w3, moe_w2, final_g):
    B, L, D = x.shape
    f32 = jnp.float32
    grid_rows = L // GRID_W
    row = jnp.repeat(jnp.arange(grid_rows, dtype=f32), GRID_W)
    col = jnp.tile(jnp.arange(GRID_W, dtype=f32), grid_rows)
    freqs = ROPE_THETA ** (-jnp.arange(ROPE_F, dtype=f32) / ROPE_F)
    ang = jnp.stack([row[:, None] * freqs, col[:, None] * freqs], axis=1)
    cos, sin = jnp.cos(ang), jnp.sin(ang)
    silu_c = jax.nn.silu(c)
    silu_cc = jax.nn.silu(c_ctx)

    for i in range(DEPTH):
        last = i == DEPTH - 1
        kind, j = i % N_MIXERS, i // N_MIXERS
        mod_l = (silu_c @ ada_w[i] + ada_b[i])[:, None, :]
        mod_c = silu_cc @ ada_w[i] + ada_b[i]
        sh1_l, sc1_l, g1_l, sh2_l, sc2_l, g2_l = jnp.split(mod_l, 6, axis=-1)
        sh1_c, sc1_c, g1_c, sh2_c, sc2_c, g2_c = jnp.split(mod_c, 6, axis=-1)

        h_l = rms_norm(x, norm1_g[i]) * (1.0 + sc1_l) + sh1_l
        h_c = rms_norm(ctx, norm1_g[i]) * (1.0 + sc1_c) + sh1_c
        if kind == 0:
            y_l, y_c = diff_attention(h_l, h_c, cos, sin, attn_w_qkv[j], attn_lq1[j], attn_lk1[j],
                                      attn_lq2[j], attn_lk2[j], attn_subln_g[j], attn_w_o[j],
                                      0.8 - 0.6 * math.exp(-0.3 * i), not last)
        elif kind == 1:
            y_l = fourier_mix(h_l, fnet_w_o[j], fnet_b_o[j])
            y_c = None if last else fourier_mix(h_c, fnet_w_o[j], fnet_b_o[j])
        else:
            hp = (hy_w_in[j], hy_conv_w[j], hy_conv_b[j], hy_f_w1[j], hy_f_b1[j], hy_f_w2[j],
                  hy_f_b2[j], hy_f_w3[j], hy_f_bias[j], hy_w_o[j])
            y_l = hyena_mixer(h_l, *hp)
            y_c = None if last else hyena_mixer(h_c, *hp)
        x = x + g1_l * y_l
        if not last:
            ctx = ctx + g1_c * y_c

        h2_l = (rms_norm(x, norm2_g[i]) * (1.0 + sc2_l) + sh2_l).reshape(B * L, D)
        if last:
            tokens = h2_l
        else:
            h2_c = (rms_norm(ctx, norm2_g[i]) * (1.0 + sc2_c) + sh2_c).reshape(-1, D)
            tokens = jnp.concatenate([h2_l, h2_c], axis=0)
        if i % 2 == 0:
            k = i // 2
            f = swiglu(tokens, ff_w1[k], ff_w3[k], ff_w2[k])
        else:
            k = i // 2
            f = moe_swiglu(tokens, moe_router[k], moe_w1[k], moe_w3[k], moe_w2[k])
        x = x + g2_l * f[:B * L].reshape(B, L, D)
        if not last:
            ctx = ctx + g2_c * f[B * L:].reshape(B, -1, D)

    return rms_norm(x, final_g)
```

```python
import functools
import math

import jax
import jax.numpy as jnp
from jax import lax
from jax.experimental import pallas as pl
from jax.experimental.pallas import tpu as pltpu

F32 = jnp.float32
BF16 = jnp.bfloat16
HIGHEST = lax.Precision.HIGHEST

DEPTH = 4
N_MIXERS = 3
GRID_W = 64
DA_HEADS = 8
DA_HEAD_DIM = 64
ROPE_F = DA_HEAD_DIM // 4
ROPE_THETA = 10000.0
FN_GROUPS = 8
HY_ORDER = 2
HY_EMB = 33
HY_BANDS = (HY_EMB - 1) // 2
HY_SIN_FREQ = 1.0
HY_FAST_DECAY = 0.3
HY_SLOW_DECAY = 1.5
HY_TARGET = 1e-2
HY_MAX_DECAY = math.log(HY_TARGET) / HY_FAST_DECAY
HY_MIN_DECAY = math.log(HY_TARGET) / HY_SLOW_DECAY
N_EXPERTS = 8
EPS = 1e-6

LANES = 128
MXU_COLS = 256
BF16_SUBLANES = 16
VMEM_LIMIT = 56 * 1024 * 1024

TOKEN_TILE = 512
FF_TILE = 512
CHAN_TILE = 256
MOE_CHUNK = 128
MOE_ROW_TILE = 512


def _cparams(sem):
    return pltpu.CompilerParams(dimension_semantics=sem, vmem_limit_bytes=VMEM_LIMIT)


def _tile(n, target, align=LANES):
    if n <= target:
        return n
    t = target // align * align
    while n % t:
        t -= align
    return t


def _norm_mod(x, g, scale, shift):
    ms = jnp.mean(x * x, axis=-1, keepdims=True)
    return (x * lax.rsqrt(ms + EPS) * g) * (1.0 + scale) + shift


def _mod_kernel(c_ref, w_ref, b_ref, o_ref):
    c = c_ref[...]
    s = (c * jax.nn.sigmoid(c)).astype(BF16)
    o_ref[...] = jnp.dot(s, w_ref[...].astype(BF16), preferred_element_type=F32) + b_ref[...]


def _modulation(c_all, ada_w, ada_b):
    depth, d, n = ada_w.shape
    rows = c_all.shape[0]
    tn = _tile(n, 1024)
    return pl.pallas_call(
        _mod_kernel,
        grid=(depth, n // tn),
        in_specs=[
            pl.BlockSpec((rows, d), lambda l, j: (0, 0)),
            pl.BlockSpec((None, d, tn), lambda l, j: (l, 0, j)),
            pl.BlockSpec((None, 1, tn), lambda l, j: (l, 0, j)),
        ],
        out_specs=pl.BlockSpec((None, rows, tn), lambda l, j: (l, 0, j)),
        out_shape=jax.ShapeDtypeStruct((depth, rows, n), F32),
        compiler_params=_cparams(("parallel", "parallel")),
        name="adaln_modulation",
    )(c_all, ada_w, ada_b.reshape(depth, 1, n))


def _norm_matmul_kernel(x_ref, mod_ref, g_ref, w_ref, *rest, n_out, rope_cols, q_cols, n_lat):
    o_ref = rest[-1]
    h = _norm_mod(x_ref[...], g_ref[...], mod_ref[1:2, :], mod_ref[0:1, :]).astype(BF16)
    if rope_cols:
        cos_ref, sa_ref, sb_ref = rest[:3]
        is_lat = pl.program_id(0) < n_lat
        cos = jnp.where(is_lat, cos_ref[...], 1.0)
        sa = jnp.where(is_lat, sa_ref[...], 0.0)
        sb = jnp.where(is_lat, sb_ref[...], 0.0)
    for c0 in range(0, n_out, MXU_COLS):
        acc = jnp.dot(h, w_ref[:, c0:c0 + MXU_COLS], preferred_element_type=F32)
        for c1 in range(c0, c0 + MXU_COLS, LANES):
            part = acc[:, c1 - c0:c1 - c0 + LANES]
            if c1 < rope_cols:
                part = (part * cos + pltpu.roll(part, LANES - ROPE_F, axis=1) * sa
                        + pltpu.roll(part, ROPE_F, axis=1) * sb)
                if c1 < q_cols:
                    part = part * (DA_HEAD_DIM ** -0.5)
            o_ref[:, c1:c1 + LANES] = part.astype(BF16)


def _norm_matmul(xs, mod, g, w, *, n_rows, rope=None, n_lat=0):
    nb, l, d = xs.shape
    n = w.shape[1]
    tm = min(TOKEN_TILE, l)
    in_specs = [
        pl.BlockSpec((None, tm, d), lambda b, i: (b, i, 0)),
        pl.BlockSpec((None, 6, d), lambda b, i: (b, 0, 0)),
        pl.BlockSpec((1, d), lambda b, i: (0, 0)),
        pl.BlockSpec((d, n), lambda b, i: (0, 0)),
    ]
    args = [xs, mod, g.reshape(1, d), w]
    rope_cols = q_cols = 0
    if rope is not None:
        rope_cols, q_cols = 2 * d, d
        in_specs += [pl.BlockSpec((tm, LANES), lambda b, i: (i, 0))] * 3
        args += list(rope)
    return pl.pallas_call(
        functools.partial(_norm_matmul_kernel, n_out=n, rope_cols=rope_cols, q_cols=q_cols, n_lat=n_lat),
        grid=(n_rows, l // tm),
        in_specs=in_specs,
        out_specs=pl.BlockSpec((None, tm, n), lambda b, i: (b, i, 0)),
        out_shape=jax.ShapeDtypeStruct((nb, l, n), BF16),
        compiler_params=_cparams(("parallel", "parallel")),
        name="norm_matmul",
    )(*args)


def _rope_tables(l):
    t = jnp.arange(l, dtype=jnp.int32)
    row = (t // GRID_W).astype(F32)
    col = (t % GRID_W).astype(F32)
    freqs = ROPE_THETA ** (-jnp.arange(ROPE_F, dtype=F32) / ROPE_F)
    lane = jnp.arange(LANES, dtype=jnp.int32)
    axis = (lane // (2 * ROPE_F)) % 2
    f = freqs[lane % ROPE_F]
    pos = jnp.where(axis[None, :] == 0, row[:, None], col[:, None])
    ang = pos * f[None, :]
    cos, sin = jnp.cos(ang), jnp.sin(ang)
    lower = (lane % (2 * ROPE_F)) < ROPE_F
    sa = jnp.where(lower[None, :], -sin, 0.0)
    sb = jnp.where(lower[None, :], 0.0, sin)
    return cos, sa, sb


def _attn_kernel(lqk_ref, g_ref, q_ref, *refs, nseg, lam_init):
    kv, o_ref = refs[:2 * nseg], refs[2 * nseg]
    q = q_ref[...]
    lq = lqk_ref[...]
    lam = (jnp.exp(jnp.sum(lq[0:1] * lq[1:2], axis=-1, keepdims=True))
           - jnp.exp(jnp.sum(lq[2:3] * lq[3:4], axis=-1, keepdims=True)) + lam_init)
    lane = lax.broadcasted_iota(jnp.int32, q.shape, 1)
    nt = (((1,), (1,)), ((), ()))
    probs = []
    for comp in range(2):
        keep = (lane < DA_HEAD_DIM) if comp == 0 else (lane >= DA_HEAD_DIM)
        qm = jnp.where(keep, q, jnp.zeros_like(q))
        s = [lax.dot_general(qm, kv[2 * i][...], nt, preferred_element_type=F32) for i in range(nseg)]
        m = functools.reduce(jnp.maximum, [jnp.max(si, axis=-1, keepdims=True) for si in s])
        e = [jnp.exp(si - m) for si in s]
        denom = functools.reduce(jnp.add, [jnp.sum(ei, axis=-1, keepdims=True) for ei in e])
        probs.append((e, 1.0 / denom))
    w0 = probs[0][1]
    w1 = probs[1][1] * lam
    o = None
    for i in range(nseg):
        a = (probs[0][0][i] * w0 - probs[1][0][i] * w1).astype(BF16)
        oi = jnp.dot(a, kv[2 * i + 1][...], preferred_element_type=F32)
        o = oi if o is None else o + oi
    ms = jnp.mean(o * o, axis=-1, keepdims=True)
    o_ref[...] = ((o * lax.rsqrt(ms + EPS) * g_ref[...]) * (1.0 - lam_init)).astype(BF16)


def _attention(qkv, lqk, subln_g, *, lam_init, batch, ctx_len, d, with_ctx):
    nb, l, _ = qkv.shape
    hd = 2 * DA_HEAD_DIM
    heads = d // hd
    tq = min(TOKEN_TILE, l)
    per_row = l // ctx_len
    g2 = subln_g.reshape(1, hd)

    def ctx_row(b):
        return batch + b // per_row

    small = [pl.BlockSpec((4, DA_HEAD_DIM), lambda b, h, i: (0, 0)),
             pl.BlockSpec((1, hd), lambda b, h, i: (0, 0))]
    lat = pl.pallas_call(
        functools.partial(_attn_kernel, nseg=2, lam_init=lam_init),
        grid=(batch, heads, l // tq),
        in_specs=small + [
            pl.BlockSpec((None, tq, hd), lambda b, h, i: (b, i, h)),
            pl.BlockSpec((None, ctx_len, hd), lambda b, h, i: (ctx_row(b), b % per_row, heads + h)),
            pl.BlockSpec((None, ctx_len, hd), lambda b, h, i: (ctx_row(b), b % per_row, 2 * heads + h)),
            pl.BlockSpec((None, l, hd), lambda b, h, i: (b, 0, heads + h)),
            pl.BlockSpec((None, l, hd), lambda b, h, i: (b, 0, 2 * heads + h)),
        ],
        out_specs=pl.BlockSpec((None, tq, hd), lambda b, h, i: (b, i, h)),
        out_shape=jax.ShapeDtypeStruct((nb, l, d), BF16),
        compiler_params=_cparams(("parallel", "parallel", "parallel")),
        name="diff_attention_latent",
    )(lqk, g2, qkv, qkv, qkv, qkv, qkv)
    if not with_ctx:
        return lat
    seqs = nb * per_row
    qkv_c = qkv.reshape(seqs, ctx_len, 3 * d)
    off = batch * per_row
    small2 = [pl.BlockSpec((4, DA_HEAD_DIM), lambda b, h: (0, 0)),
              pl.BlockSpec((1, hd), lambda b, h: (0, 0))]
    out = pl.pallas_call(
        functools.partial(_attn_kernel_aliased, nseg=1, lam_init=lam_init),
        grid=(batch, heads),
        in_specs=small2 + [
            pl.BlockSpec((None, ctx_len, hd), lambda b, h: (off + b, 0, h)),
            pl.BlockSpec((None, ctx_len, hd), lambda b, h: (off + b, 0, heads + h)),
            pl.BlockSpec((None, ctx_len, hd), lambda b, h: (off + b, 0, 2 * heads + h)),
            pl.BlockSpec(memory_space=pl.ANY),
        ],
        out_specs=pl.BlockSpec((None, ctx_len, hd), lambda b, h: (off + b, 0, h)),
        out_shape=jax.ShapeDtypeStruct((seqs, ctx_len, d), BF16),
        input_output_aliases={5: 0},
        compiler_params=_cparams(("parallel", "parallel")),
        name="diff_attention_context",
    )(lqk, g2, qkv_c, qkv_c, qkv_c, lat.reshape(seqs, ctx_len, d))
    return out.reshape(nb, l, d)


def _attn_kernel_aliased(lqk_ref, g_ref, q_ref, k_ref, v_ref, prev_ref, o_ref, *, nseg, lam_init):
    del prev_ref
    _attn_kernel(lqk_ref, g_ref, q_ref, k_ref, v_ref, o_ref, nseg=nseg, lam_init=lam_init)


def _proj_residual_kernel(a_ref, w_ref, *rest, gate_row, has_bias):
    if has_bias:
        b_ref, mod_ref, x_ref, o_ref = rest
    else:
        mod_ref, x_ref, o_ref = rest
    y = jnp.dot(a_ref[...], w_ref[...], preferred_element_type=F32)
    if has_bias:
        y = y + b_ref[...]
    o_ref[...] = x_ref[...] + mod_ref[gate_row:gate_row + 1, :] * y


def _proj_residual(a, w, mod, xs, *, n_rows, gate_row, bias=None):
    nb, l, d = xs.shape
    k = a.shape[-1]
    tm = min(TOKEN_TILE, l)
    in_specs = [pl.BlockSpec((None, tm, k), lambda b, i: (b, i, 0)),
                pl.BlockSpec((k, d), lambda b, i: (0, 0))]
    args = [a, w]
    if bias is not None:
        in_specs.append(pl.BlockSpec((1, d), lambda b, i: (0, 0)))
        args.append(bias.reshape(1, d))
    in_specs += [pl.BlockSpec((None, 6, d), lambda b, i: (b, 0, 0)),
                 pl.BlockSpec((None, tm, d), lambda b, i: (b, i, 0))]
    args += [mod, xs]
    return pl.pallas_call(
        functools.partial(_proj_residual_kernel, gate_row=gate_row, has_bias=bias is not None),
        grid=(n_rows, l // tm),
        in_specs=in_specs,
        out_specs=pl.BlockSpec((None, tm, d), lambda b, i: (b, i, 0)),
        out_shape=jax.ShapeDtypeStruct((nb, l, d), F32),
        input_output_aliases={len(args) - 1: 0},
        compiler_params=_cparams(("parallel", "parallel")),
        name="proj_residual",
    )(*args)


def _swiglu_partial(h, w1_ref, w3_ref, w2_ref):
    a = jnp.dot(h, w1_ref[...], preferred_element_type=F32)
    b = jnp.dot(h, w3_ref[...], preferred_element_type=F32)
    g = ((a * jax.nn.sigmoid(a)) * b).astype(BF16)
    return jnp.dot(g, w2_ref[...], preferred_element_type=F32)


def _ffn_kernel(x_ref, mod_ref, g_ref, w1_ref, w3_ref, w2_ref, o_ref, h_ref, acc_ref):
    f = pl.program_id(2)

    @pl.when(f == 0)
    def _():
        h_ref[...] = _norm_mod(x_ref[...], g_ref[...], mod_ref[4:5, :], mod_ref[3:4, :]).astype(BF16)
        acc_ref[...] = jnp.zeros_like(acc_ref)

    acc_ref[...] += _swiglu_partial(h_ref[...], w1_ref, w3_ref, w2_ref)

    @pl.when(f == pl.num_programs(2) - 1)
    def _():
        o_ref[...] = x_ref[...] + mod_ref[5:6, :] * acc_ref[...]


def _ffn(xs, mod, g, w1, w3, w2, *, n_rows):
    nb, l, d = xs.shape
    ff = w1.shape[1]
    tm = min(2 * TOKEN_TILE, l)
    tf = _tile(ff, FF_TILE)
    return pl.pallas_call(
        _ffn_kernel,
        grid=(n_rows, l // tm, ff // tf),
        in_specs=[
            pl.BlockSpec((None, tm, d), lambda b, i, f: (b, i, 0)),
            pl.BlockSpec((None, 6, d), lambda b, i, f: (b, 0, 0)),
            pl.BlockSpec((1, d), lambda b, i, f: (0, 0)),
            pl.BlockSpec((d, tf), lambda b, i, f: (0, f)),
            pl.BlockSpec((d, tf), lambda b, i, f: (0, f)),
            pl.BlockSpec((tf, d), lambda b, i, f: (f, 0)),
        ],
        out_specs=pl.BlockSpec((None, tm, d), lambda b, i, f: (b, i, 0)),
        out_shape=jax.ShapeDtypeStruct((nb, l, d), F32),
        scratch_shapes=[pltpu.VMEM((tm, d), BF16), pltpu.VMEM((tm, d), F32)],
        input_output_aliases={0: 0},
        compiler_params=_cparams(("parallel", "parallel", "arbitrary")),
        name="dense_swiglu",
    )(xs, mod, g.reshape(1, d), w1, w3, w2)


def _dft_cos_sin(n_out, n_in, num, den, scale):
    k = jnp.arange(n_out, dtype=jnp.int32)[:, None]
    s = jnp.arange(n_in, dtype=jnp.int32)[None, :]
    ang = (num(k, s) % den).astype(F32) * (2.0 * math.pi / den)
    return (jnp.cos(ang) * scale).astype(BF16), (jnp.sin(ang) * scale).astype(BF16)


def _fnet_chan_kernel(x_ref, mod_ref, g_ref, cs_ref, a_ref, b_ref, *, gd):
    h = _norm_mod(x_ref[...], g_ref[...], mod_ref[1:2, :], mod_ref[0:1, :]).astype(BF16)
    for c0 in range(0, h.shape[1], gd):
        r = jnp.dot(h[:, c0:c0 + gd], cs_ref[...], preferred_element_type=F32)
        a_ref[:, c0:c0 + gd] = r[:, :gd].astype(BF16)
        b_ref[:, c0:c0 + gd] = r[:, gd:].astype(BF16)


def _fnet_seq_kernel(c_ref, s_ref, a_ref, b_ref, w_ref, bias_ref, mod_ref, x_ref, o_ref):
    f = (jnp.dot(c_ref[...], a_ref[...], preferred_element_type=F32)
         - jnp.dot(s_ref[...], b_ref[...], preferred_element_type=F32)).astype(BF16)
    y = jnp.dot(f, w_ref[...], preferred_element_type=F32) + bias_ref[...]
    o_ref[...] = x_ref[...] + mod_ref[2:3, :] * y


def _fnet_seq(a, b, xs, mod, w_o, b_o, *, seq, n_seq, seq_off, mod_row):
    nb, l, d = xs.shape
    seqs = nb * l // seq
    tm = min(TOKEN_TILE, seq)
    cm, sm = _dft_cos_sin(seq, seq, lambda k, s: k * s, seq, seq ** -0.5)
    view = lambda t: t.reshape(seqs, seq, d)
    out = pl.pallas_call(
        _fnet_seq_kernel,
        grid=(n_seq, seq // tm),
        in_specs=[
            pl.BlockSpec((tm, seq), lambda b, i: (i, 0)),
            pl.BlockSpec((tm, seq), lambda b, i: (i, 0)),
            pl.BlockSpec((None, seq, d), lambda b, i: (seq_off + b, 0, 0)),
            pl.BlockSpec((None, seq, d), lambda b, i: (seq_off + b, 0, 0)),
            pl.BlockSpec((d, d), lambda b, i: (0, 0)),
            pl.BlockSpec((1, d), lambda b, i: (0, 0)),
            pl.BlockSpec((None, 6, d), lambda b, i: (mod_row(b), 0, 0)),
            pl.BlockSpec((None, tm, d), lambda b, i: (seq_off + b, i, 0)),
        ],
        out_specs=pl.BlockSpec((None, tm, d), lambda b, i: (seq_off + b, i, 0)),
        out_shape=jax.ShapeDtypeStruct((seqs, seq, d), F32),
        input_output_aliases={7: 0},
        compiler_params=_cparams(("parallel", "parallel")),
        name="fourier_seq_proj_residual",
    )(cm, sm, view(a), view(b), w_o, b_o.reshape(1, d), mod, view(xs))
    return out.reshape(nb, l, d)


def _fourier_layer(xs, mod, g, w_o, b_o, *, batch, ctx_len, with_ctx):
    nb, l, d = xs.shape
    gd = d // FN_GROUPS
    tm = min(TOKEN_TILE, l)
    n_rows = nb if with_ctx else batch
    cd, sd = _dft_cos_sin(gd, gd, lambda k, s: k * s, gd, gd ** -0.5)
    a, b = pl.pallas_call(
        functools.partial(_fnet_chan_kernel, gd=gd),
        grid=(n_rows, l // tm),
        in_specs=[
            pl.BlockSpec((None, tm, d), lambda r, i: (r, i, 0)),
            pl.BlockSpec((None, 6, d), lambda r, i: (r, 0, 0)),
            pl.BlockSpec((1, d), lambda r, i: (0, 0)),
            pl.BlockSpec((gd, 2 * gd), lambda r, i: (0, 0)),
        ],
        out_specs=[pl.BlockSpec((None, tm, d), lambda r, i: (r, i, 0))] * 2,
        out_shape=[jax.ShapeDtypeStruct((nb, l, d), BF16)] * 2,
        compiler_params=_cparams(("parallel", "parallel")),
        name="fourier_channel_dft",
    )(xs, mod, g.reshape(1, d), jnp.concatenate([cd, sd], axis=1))
    w = w_o.astype(BF16)
    xs = _fnet_seq(a, b, xs, mod, w, b_o, seq=l, n_seq=batch, seq_off=0, mod_row=lambda r: r)
    if with_ctx:
        xs = _fnet_seq(a, b, xs, mod, w, b_o, seq=ctx_len, n_seq=batch, seq_off=batch * (l // ctx_len),
                       mod_row=lambda r: batch)
    return xs


def _hyena_filter_kernel(emb_ref, w1_ref, b1_ref, w2_ref, b2_ref, w3f_ref, w3b_ref, dec_ref, c0_ref, s0_ref,
                         kr_ref, ki_ref, *, scale):
    dot = functools.partial(jnp.dot, precision=HIGHEST, preferred_element_type=F32)
    hdn = jnp.sin(HY_SIN_FREQ * (dot(emb_ref[...], w1_ref[...]) + b1_ref[...]))
    hdn = jnp.sin(HY_SIN_FREQ * (dot(hdn, w2_ref[...]) + b2_ref[...]))
    dec = dec_ref[...] * scale
    fwd = dot(hdn, w3f_ref[...]) * dec
    bwd = dot(hdn, w3b_ref[...]) * dec
    row = lax.broadcasted_iota(jnp.int32, bwd.shape, 0)
    bwd = jnp.where(row == 0, 0.0, bwd)
    kr_ref[...] = jnp.dot(c0_ref[...], (fwd + bwd).astype(BF16), preferred_element_type=F32).astype(BF16)
    ki_ref[...] = jnp.dot(s0_ref[...], (bwd - fwd).astype(BF16), preferred_element_type=F32).astype(BF16)


def _hyena_spectra(seq, d, f_w1, f_b1, f_w2, f_b2, f_w3):
    n2 = 2 * seq
    hid = f_w1.shape[1]
    pad = LANES
    t01 = jnp.linspace(0.0, 1.0, seq, dtype=F32)[:, None]
    wv = 2.0 * math.pi * jnp.arange(seq, dtype=F32) / seq
    bands = jnp.linspace(1e-4, HY_BANDS - 1, HY_BANDS, dtype=F32)
    fw = wv[:, None] * bands[None, :]
    emb = jnp.concatenate([t01, jnp.cos(fw), -jnp.sin(fw)], axis=-1)
    emb = jnp.pad(emb, ((0, 0), (0, pad - HY_EMB)))
    deltas = jnp.abs(jnp.linspace(HY_MIN_DECAY, HY_MAX_DECAY, d, dtype=F32))
    decay = jnp.exp(-t01 * deltas[None, :])
    w1 = jnp.pad(f_w1, ((0, pad - HY_EMB), (0, pad - hid)))
    b1 = jnp.pad(f_b1, (0, pad - hid)).reshape(1, pad)
    w2 = jnp.pad(f_w2, ((0, pad - hid), (0, pad - hid)))
    b2 = jnp.pad(f_b2, (0, pad - hid)).reshape(1, pad)
    w3 = jnp.pad(f_w3, ((0, pad - hid), (0, 0)))
    c0, s0 = _dft_cos_sin(seq, seq, lambda k, m: (2 * k + 1) * m, 2 * n2, 1.0)
    tc = min(CHAN_TILE, d)
    tk = min(TOKEN_TILE, seq)
    nct = d // tc
    full = lambda shape: pl.BlockSpec(shape, lambda n, j, k: (0,) * len(shape))
    kr, ki = pl.pallas_call(
        functools.partial(_hyena_filter_kernel, scale=2.0 / n2),
        grid=(HY_ORDER, nct, seq // tk),
        in_specs=[
            full((seq, pad)), full((pad, pad)), full((1, pad)), full((pad, pad)), full((1, pad)),
            pl.BlockSpec((pad, tc), lambda n, j, k: (0, (2 * n) * nct + j)),
            pl.BlockSpec((pad, tc), lambda n, j, k: (0, (2 * n + 1) * nct + j)),
            pl.BlockSpec((seq, tc), lambda n, j, k: (0, j)),
            pl.BlockSpec((tk, seq), lambda n, j, k: (k, 0)),
            pl.BlockSpec((tk, seq), lambda n, j, k: (k, 0)),
        ],
        out_specs=[pl.BlockSpec((None, tk, tc), lambda n, j, k: (n, k, j))] * 2,
        out_shape=[jax.ShapeDtypeStruct((HY_ORDER, seq, d), BF16)] * 2,
        compiler_params=_cparams(("parallel", "parallel", "parallel")),
        name="hyena_filter_spectrum",
    )(emb, w1, b1, w2, b2, w3, w3, decay, c0, s0)
    return jnp.stack([kr, ki], axis=1).reshape(2 * HY_ORDER, seq, d)


def _hyena_conv_kernel(x1_ref, x2_ref, v_ref, cw1_ref, cw2_ref, cwv_ref, cb1_ref, cb2_ref, cbv_ref,
                       k_ref, fb_ref, cs_ref, ss_ref, *rest):
    o_ref = rest[-1]
    seq = v_ref.shape[0]

    def short_conv(u_ref, w_ref, b_ref):
        u = u_ref[...].astype(F32)
        row = lax.broadcasted_iota(jnp.int32, u.shape, 0)
        prev = jnp.where(row == 0, 0.0, pltpu.roll(u, 1, axis=0))
        nxt = jnp.where(row == seq - 1, 0.0, pltpu.roll(u, seq - 1, axis=0))
        return prev * w_ref[0:1, :] + u * w_ref[1:2, :] + nxt * w_ref[2:3, :] + b_ref[...]

    gates = (short_conv(x1_ref, cw1_ref, cb1_ref), short_conv(x2_ref, cw2_ref, cb2_ref))
    z = short_conv(v_ref, cwv_ref, cbv_ref)
    for n in range(HY_ORDER):
        zb = z.astype(BF16)
        p = jnp.dot(cs_ref[...], zb, preferred_element_type=F32)
        q = jnp.dot(ss_ref[...], zb, preferred_element_type=F32)
        kr = k_ref[2 * n].astype(F32)
        ki = k_ref[2 * n + 1].astype(F32)
        yr = (p * kr + q * ki).astype(BF16)
        yi = (p * ki - q * kr).astype(BF16)
        y = (jnp.dot(cs_ref[...], yr, preferred_element_type=F32)
             - jnp.dot(ss_ref[...], yi, preferred_element_type=F32))
        z = gates[n] * (y + fb_ref[n:n + 1, :] * z)
    o_ref[...] = z.astype(BF16)


def _hyena_conv(u, spectra, conv_w, conv_b, f_bias, prev, *, seq, n_seq, seq_off):
    nb, l, d3 = u.shape
    d = d3 // 3
    seqs = nb * l // seq
    tc = min(CHAN_TILE, d)
    nct = d // tc
    cs, ss = _dft_cos_sin(seq, seq, lambda k, s: (2 * k + 1) * (2 * s + 1), 8 * seq, 1.0)
    uv = u.reshape(seqs, seq, d3)
    cb = conv_b.reshape(1, d3)
    u_spec = lambda part: pl.BlockSpec((None, seq, tc), lambda j, r: (seq_off + r, 0, part * nct + j))
    w_spec = lambda part: pl.BlockSpec((3, tc), lambda j, r: (0, part * nct + j))
    b_spec = lambda part: pl.BlockSpec((1, tc), lambda j, r: (0, part * nct + j))
    in_specs = [u_spec(0), u_spec(1), u_spec(2), w_spec(0), w_spec(1), w_spec(2), b_spec(0), b_spec(1), b_spec(2),
                pl.BlockSpec((2 * HY_ORDER, seq, tc), lambda j, r: (0, 0, j), pipeline_mode=pl.Buffered(1)),
                pl.BlockSpec((HY_ORDER, tc), lambda j, r: (0, j)),
                pl.BlockSpec((seq, seq), lambda j, r: (0, 0), pipeline_mode=pl.Buffered(1)),
                pl.BlockSpec((seq, seq), lambda j, r: (0, 0), pipeline_mode=pl.Buffered(1))]
    args = [uv, uv, uv, conv_w, conv_w, conv_w, cb, cb, cb, spectra, f_bias, cs, ss]
    aliases = {}
    if prev is not None:
        in_specs.append(pl.BlockSpec(memory_space=pl.ANY))
        args.append(prev.reshape(seqs, seq, d))
        aliases = {len(args) - 1: 0}
    out = pl.pallas_call(
        _hyena_conv_kernel,
        grid=(nct, n_seq),
        in_specs=in_specs,
        out_specs=pl.BlockSpec((None, seq, tc), lambda j, r: (seq_off + r, 0, j)),
        out_shape=jax.ShapeDtypeStruct((seqs, seq, d), BF16),
        input_output_aliases=aliases,
        compiler_params=_cparams(("parallel", "parallel")),
        name="hyena_long_conv",
    )(*args)
    return out.reshape(nb, l, d)


def _round_up(v, m):
    return (v + m - 1) // m * m


def _moe_route_kernel(x_ref, mod_ref, g_ref, rt_ref, xs_hbm, info_ref, base_ref, cnt_ref, used_ref,
                      run_ref, stage_ref, zero_ref, sem, *, tiles_per_row, n_tiles, cap_rows, row_tile):
    tm, d = x_ref.shape
    n_chunks = tm // MOE_CHUNK
    i = pl.program_id(0) * tiles_per_row + pl.program_id(1)

    @pl.when(i == 0)
    def _():
        for e in range(N_EXPERTS):
            run_ref[e] = 0

    xn = _norm_mod(x_ref[...], g_ref[...], mod_ref[4:5, :], mod_ref[3:4, :])
    hb = xn.astype(BF16)
    lt = lax.dot_general(rt_ref[...], xn, (((1,), (1,)), ((), ())), precision=HIGHEST, preferred_element_type=F32)
    eio = lax.broadcasted_iota(jnp.int32, lt.shape, 0)
    m1 = jnp.max(lt, axis=0, keepdims=True)
    i1 = jnp.min(jnp.where(lt == m1, eio, N_EXPERTS), axis=0, keepdims=True)
    mask1 = eio == i1
    l2 = jnp.where(mask1, -jnp.inf, lt)
    m2 = jnp.max(l2, axis=0, keepdims=True)
    i2 = jnp.min(jnp.where(l2 == m2, eio, N_EXPERTS), axis=0, keepdims=True)
    mask2 = eio == i2
    ee = jnp.exp(m2 - m1)
    gate = jnp.where(mask1, 1.0 / (1.0 + ee), jnp.where(mask2, ee / (1.0 + ee), 0.0))
    sel = jnp.where(mask1 | mask2, 1.0, 0.0)
    t0 = lax.broadcasted_iota(jnp.int32, (tm, tm), 0)
    t1 = lax.broadcasted_iota(jnp.int32, (tm, tm), 1)
    before = jnp.where(t0 < t1, 1.0, 0.0).astype(BF16)
    rank = jnp.dot(sel.astype(BF16), before, preferred_element_type=F32)
    packed = jnp.concatenate([rank, gate, jnp.zeros((LANES - 2 * N_EXPERTS, tm), F32)], axis=0)
    info_ref[...] = packed.T

    def chunk_copy(e, jc, base):
        dst = pl.multiple_of(e * cap_rows + base + jc * MOE_CHUNK, BF16_SUBLANES)
        return pltpu.make_async_copy(stage_ref.at[e, jc], xs_hbm.at[pl.ds(dst, MOE_CHUNK)], sem.at[e, jc])

    counts, bases = [], []
    for e in range(N_EXPERTS):
        c = jnp.sum(sel[e:e + 1, :]).astype(jnp.int32)
        base = run_ref[e]
        counts.append(c)
        bases.append(base)
        base_ref[i * N_EXPERTS + e] = base
        cnt_ref[i * N_EXPERTS + e] = c
        for jc in range(n_chunks):
            @pl.when(jc * MOE_CHUNK < c)
            def _():
                r = (lax.broadcasted_iota(jnp.int32, (MOE_CHUNK, tm), 0) + jc * MOE_CHUNK).astype(F32)
                onehot = jnp.where((rank[e:e + 1, :] == r) & (sel[e:e + 1, :] > 0.0), 1.0, 0.0).astype(BF16)
                stage_ref[e, jc] = jnp.dot(onehot, hb, preferred_element_type=F32).astype(BF16)
                chunk_copy(e, jc, base).start()
        run_ref[e] = base + _round_up(c, BF16_SUBLANES)

    for e in range(N_EXPERTS):
        for jc in range(n_chunks):
            @pl.when(jc * MOE_CHUNK < counts[e])
            def _():
                chunk_copy(e, jc, bases[e]).wait()

    @pl.when(i == n_tiles - 1)
    def _():
        zero_ref[...] = jnp.zeros_like(zero_ref)
        for e in range(N_EXPERTS):
            used = run_ref[e]
            used_ref[e] = used
            end = jnp.where(used > 0, _round_up(used + MOE_CHUNK, row_tile), 0)
            for jz in range(row_tile // MOE_CHUNK + 1):
                start = used + jz * MOE_CHUNK

                @pl.when(start < end)
                def _():
                    dst = pl.multiple_of(e * cap_rows + start, BF16_SUBLANES)
                    cp = pltpu.make_async_copy(zero_ref, xs_hbm.at[pl.ds(dst, MOE_CHUNK)], sem.at[e, 0])
                    cp.start()
                    cp.wait()


def _moe_route(xs, mod, g, router, *, n_rows, cap_rows, row_tile):
    nb, l, d = xs.shape
    tm = min(TOKEN_TILE, l)
    tpr = l // tm
    n_tiles = n_rows * tpr
    n_chunks = tm // MOE_CHUNK
    smem = pl.BlockSpec(memory_space=pltpu.SMEM)
    return pl.pallas_call(
        functools.partial(_moe_route_kernel, tiles_per_row=tpr, n_tiles=n_tiles, cap_rows=cap_rows,
                          row_tile=row_tile),
        grid=(n_rows, tpr),
        in_specs=[
            pl.BlockSpec((None, tm, d), lambda b, i: (b, i, 0)),
            pl.BlockSpec((None, 6, d), lambda b, i: (b, 0, 0)),
            pl.BlockSpec((1, d), lambda b, i: (0, 0)),
            pl.BlockSpec((N_EXPERTS, d), lambda b, i: (0, 0)),
        ],
        out_specs=[
            pl.BlockSpec(memory_space=pl.ANY),
            pl.BlockSpec((None, tm, LANES), lambda b, i: (b * tpr + i, 0, 0)),
            smem, smem, smem,
        ],
        out_shape=[
            jax.ShapeDtypeStruct((N_EXPERTS * cap_rows, d), BF16),
            jax.ShapeDtypeStruct((n_tiles, tm, LANES), F32),
            jax.ShapeDtypeStruct((n_tiles * N_EXPERTS,), jnp.int32),
            jax.ShapeDtypeStruct((n_tiles * N_EXPERTS,), jnp.int32),
            jax.ShapeDtypeStruct((N_EXPERTS,), jnp.int32),
        ],
        scratch_shapes=[
            pltpu.SMEM((N_EXPERTS,), jnp.int32),
            pltpu.VMEM((N_EXPERTS, n_chunks, MOE_CHUNK, d), BF16),
            pltpu.VMEM((MOE_CHUNK, d), BF16),
            pltpu.SemaphoreType.DMA((N_EXPERTS, n_chunks)),
        ],
        compiler_params=_cparams(("arbitrary", "arbitrary")),
        name="moe_route_dispatch",
    )(xs, mod, g.reshape(1, d), router.T)


def _moe_expert_kernel(te_ref, tr_ref, ta_ref, x_ref, w1_ref, w3_ref, w2_ref, o_ref, acc_ref):
    del te_ref, tr_ref
    f = pl.program_id(1)

    @pl.when(ta_ref[pl.program_id(0)] == 1)
    def _():
        @pl.when(f == 0)
        def _():
            acc_ref[...] = jnp.zeros_like(acc_ref)

        acc_ref[...] += _swiglu_partial(x_ref[...], w1_ref, w3_ref, w2_ref)

        @pl.when(f == pl.num_programs(1) - 1)
        def _():
            o_ref[...] = acc_ref[...].astype(BF16)


def _moe_experts(xsorted, tile_expert, tile_row, tile_active, w1, w3, w2, *, row_tile):
    rows, d = xsorted.shape
    ff = w1.shape[2]
    tf = _tile(ff, FF_TILE)
    nf = ff // tf
    n_tiles = tile_expert.shape[0]

    def fidx(g, f, ta):
        return jnp.where(ta[g] == 1, f, nf - 1)

    grid_spec = pltpu.PrefetchScalarGridSpec(
        num_scalar_prefetch=3,
        grid=(n_tiles, nf),
        in_specs=[
            pl.BlockSpec((row_tile, d), lambda g, f, te, tr, ta: (tr[g], 0)),
            pl.BlockSpec((None, d, tf), lambda g, f, te, tr, ta: (te[g], 0, fidx(g, f, ta))),
            pl.BlockSpec((None, d, tf), lambda g, f, te, tr, ta: (te[g], 0, fidx(g, f, ta))),
            pl.BlockSpec((None, tf, d), lambda g, f, te, tr, ta: (te[g], fidx(g, f, ta), 0)),
        ],
        out_specs=pl.BlockSpec((row_tile, d), lambda g, f, te, tr, ta: (tr[g], 0)),
        scratch_shapes=[pltpu.VMEM((row_tile, d), F32)],
    )
    return pl.pallas_call(
        _moe_expert_kernel,
        grid_spec=grid_spec,
        out_shape=jax.ShapeDtypeStruct((rows, d), BF16),
        compiler_params=_cparams(("arbitrary", "arbitrary")),
        name="moe_expert_swiglu",
    )(tile_expert, tile_row, tile_active, xsorted, w1, w3, w2)


def _moe_combine_kernel(base_ref, cnt_ref, x_ref, mod_ref, info_ref, y_hbm, *rest, tiles_per_row, cap_rows,
                        final_norm):
    if final_norm:
        fg_ref, o_ref, acc_ref, buf_ref, sem = rest
    else:
        o_ref, acc_ref, buf_ref, sem = rest
    tm, d = x_ref.shape
    n_chunks = tm // MOE_CHUNK
    i = pl.program_id(0) * tiles_per_row + pl.program_id(1)

    def chunk_copy(e, jc):
        src = pl.multiple_of(e * cap_rows + base_ref[i * N_EXPERTS + e] + jc * MOE_CHUNK, BF16_SUBLANES)
        return pltpu.make_async_copy(y_hbm.at[pl.ds(src, MOE_CHUNK)], buf_ref.at[e, jc], sem.at[e, jc])

    counts = [cnt_ref[i * N_EXPERTS + e] for e in range(N_EXPERTS)]
    for e in range(N_EXPERTS):
        for jc in range(n_chunks):
            @pl.when(jc * MOE_CHUNK < counts[e])
            def _():
                chunk_copy(e, jc).start()

    acc_ref[...] = jnp.zeros_like(acc_ref)
    info = info_ref[...]
    for e in range(N_EXPERTS):
        rank = info[:, e:e + 1]
        gate = info[:, N_EXPERTS + e:N_EXPERTS + e + 1]
        for jc in range(n_chunks):
            @pl.when(jc * MOE_CHUNK < counts[e])
            def _():
                chunk_copy(e, jc).wait()
                r = (lax.broadcasted_iota(jnp.int32, (tm, MOE_CHUNK), 1) + jc * MOE_CHUNK).astype(F32)
                onehot = jnp.where((rank == r) & (gate > 0.0), 1.0, 0.0).astype(BF16)
                acc_ref[...] += gate * jnp.dot(onehot, buf_ref[e, jc], preferred_element_type=F32)

    out = x_ref[...] + mod_ref[5:6, :] * acc_ref[...]
    if final_norm:
        ms = jnp.mean(out * out, axis=-1, keepdims=True)
        out = out * lax.rsqrt(ms + EPS) * fg_ref[...]
    o_ref[...] = out


def _moe_combine(xs, mod, info, base, cnt, ysorted, *, n_rows, cap_rows, final_g=None):
    nb, l, d = xs.shape
    tm = min(TOKEN_TILE, l)
    tpr = l // tm
    n_chunks = tm // MOE_CHUNK
    final_norm = final_g is not None
    in_specs = [
        pl.BlockSpec((None, tm, d), lambda b, i, bs, ct: (b, i, 0)),
        pl.BlockSpec((None, 6, d), lambda b, i, bs, ct: (b, 0, 0)),
        pl.BlockSpec((None, tm, LANES), lambda b, i, bs, ct: (b * tpr + i, 0, 0)),
        pl.BlockSpec(memory_space=pl.ANY),
    ]
    args = [xs, mod, info, ysorted]
    if final_norm:
        in_specs.append(pl.BlockSpec((1, d), lambda b, i, bs, ct: (0, 0)))
        args.append(final_g.reshape(1, d))
    grid_spec = pltpu.PrefetchScalarGridSpec(
        num_scalar_prefetch=2,
        grid=(n_rows, tpr),
        in_specs=in_specs,
        out_specs=pl.BlockSpec((None, tm, d), lambda b, i, bs, ct: (b, i, 0)),
        scratch_shapes=[
            pltpu.VMEM((tm, d), F32),
            pltpu.VMEM((N_EXPERTS, n_chunks, MOE_CHUNK, d), BF16),
            pltpu.SemaphoreType.DMA((N_EXPERTS, n_chunks)),
        ],
    )
    return pl.pallas_call(
        functools.partial(_moe_combine_kernel, tiles_per_row=tpr, cap_rows=cap_rows, final_norm=final_norm),
        grid_spec=grid_spec,
        out_shape=jax.ShapeDtypeStruct((n_rows if final_norm else nb, l, d), F32),
        input_output_aliases={} if final_norm else {2: 0},
        compiler_params=_cparams(("parallel", "parallel")),
        name="moe_combine",
    )(base, cnt, *args)


def _moe_layer(xs, mod, g, router, w1, w3, w2, *, n_rows, final_g=None):
    nb, l, d = xs.shape
    tm = min(TOKEN_TILE, l)
    row_tile = MOE_ROW_TILE
    n_tok = n_rows * l
    n_tiles = n_tok // tm
    cap_rows = _round_up(n_tok + BF16_SUBLANES * n_tiles + 2 * MOE_CHUNK, row_tile) + row_tile
    xsorted, info, base, cnt, used = _moe_route(xs, mod, g, router, n_rows=n_rows, cap_rows=cap_rows,
                                                row_tile=row_tile)
    per_e = jnp.where(used > 0, (used + MOE_CHUNK + row_tile - 1) // row_tile, 0)
    ends = jnp.cumsum(per_e)
    total = ends[-1]
    max_tiles = (2 * n_tok + (BF16_SUBLANES - 1) * N_EXPERTS * n_tiles) // row_tile + 2 * N_EXPERTS
    gidx = jnp.minimum(jnp.arange(max_tiles, dtype=jnp.int32), total - 1)
    te = jnp.searchsorted(ends, gidx, side="right").astype(jnp.int32)
    tr = te * (cap_rows // row_tile) + gidx - (ends[te] - per_e[te])
    ta = (jnp.arange(max_tiles, dtype=jnp.int32) < total).astype(jnp.int32)
    ysorted = _moe_experts(xsorted, te, tr.astype(jnp.int32), ta, w1, w3, w2, row_tile=row_tile)
    return _moe_combine(xs, mod, info, base, cnt, ysorted, n_rows=n_rows, cap_rows=cap_rows, final_g=final_g)


def kernel(x, c, ctx, c_ctx, ada_w, ada_b, norm1_g, norm2_g, attn_w_qkv, attn_lq1, attn_lk1, attn_lq2, attn_lk2, attn_subln_g, attn_w_o, fnet_w_o, fnet_b_o, hy_w_in, hy_conv_w, hy_conv_b, hy_f_w1, hy_f_b1, hy_f_w2, hy_f_b2, hy_f_w3, hy_f_bias, hy_w_o, ff_w1, ff_w3, ff_w2, moe_router, moe_w1, moe_w3, moe_w2, final_g):
    batch, l, d = x.shape
    ctx_len = ctx.shape[1]
    assert d == DA_HEADS * 2 * DA_HEAD_DIM and l % ctx_len == 0 and (batch * ctx_len) % l == 0
    n_ctx_rows = batch * ctx_len // l
    nb = batch + n_ctx_rows
    xs = jnp.concatenate([x, ctx.reshape(n_ctx_rows, l, d)], axis=0)

    pad = _round_up(batch + 1, 8) - (batch + 1)
    c_all = jnp.concatenate([c, c_ctx[None, :], jnp.zeros((pad, d), F32)], axis=0)
    mod_all = _modulation(c_all, ada_w, ada_b)
    rope = _rope_tables(l)

    for i in range(DEPTH):
        last = i == DEPTH - 1
        kind, j = i % N_MIXERS, i // N_MIXERS
        n_rows = batch if last else nb
        m = mod_all[i]
        mod = jnp.concatenate([m[:batch], jnp.broadcast_to(m[batch:batch + 1], (n_ctx_rows, 6 * d))], axis=0)
        mod = mod.reshape(nb, 6, d)

        if kind == 0:
            lam_init = 0.8 - 0.6 * math.exp(-0.3 * i)
            qkv = _norm_matmul(xs, mod, norm1_g[i], attn_w_qkv[j].astype(BF16), n_rows=nb, rope=rope, n_lat=batch)
            lqk = jnp.stack([attn_lq1[j], attn_lk1[j], attn_lq2[j], attn_lk2[j]], axis=0)
            o = _attention(qkv, lqk, attn_subln_g[j], lam_init=lam_init, batch=batch, ctx_len=ctx_len, d=d,
                           with_ctx=not last)
            xs = _proj_residual(o, attn_w_o[j].astype(BF16), mod, xs, n_rows=n_rows, gate_row=2)
        elif kind == 1:
            xs = _fourier_layer(xs, mod, norm1_g[i], fnet_w_o[j], fnet_b_o[j], batch=batch, ctx_len=ctx_len,
                                with_ctx=not last)
        else:
            u = _norm_matmul(xs, mod, norm1_g[i], hy_w_in[j].astype(BF16), n_rows=n_rows)
            fp = (hy_f_w1[j], hy_f_b1[j], hy_f_w2[j], hy_f_b2[j], hy_f_w3[j])
            z = _hyena_conv(u, _hyena_spectra(l, d, *fp), hy_conv_w[j], hy_conv_b[j], hy_f_bias[j], None,
                            seq=l, n_seq=batch, seq_off=0)
            if not last:
                z = _hyena_conv(u, _hyena_spectra(ctx_len, d, *fp), hy_conv_w[j], hy_conv_b[j], hy_f_bias[j], z,
                                seq=ctx_len, n_seq=batch, seq_off=batch * (l // ctx_len))
            xs = _proj_residual(z, hy_w_o[j].astype(BF16), mod, xs, n_rows=n_rows, gate_row=2)

        k = i // 2
        if i % 2 == 0:
            xs = _ffn(xs, mod, norm2_g[i], ff_w1[k].astype(BF16), ff_w3[k].astype(BF16), ff_w2[k].astype(BF16),
                      n_rows=n_rows)
        else:
            xs = _moe_layer(xs, mod, norm2_g[i], moe_router[k], moe_w1[k].astype(BF16), moe_w3[k].astype(BF16),
                            moe_w2[k].astype(BF16), n_rows=n_rows, final_g=final_g if last else None)
    return xs
```

```python
import functools
import math

import jax
import jax.numpy as jnp
from jax import lax
from jax.experimental import pallas as pl
from jax.experimental.pallas import tpu as pltpu

F32 = jnp.float32
BF16 = jnp.bfloat16
HIGHEST = lax.Precision.HIGHEST

DEPTH = 4
N_MIXERS = 3
GRID_W = 64
DA_HEADS = 8
DA_HEAD_DIM = 64
ROPE_F = DA_HEAD_DIM // 4
ROPE_THETA = 10000.0
FN_GROUPS = 8
HY_ORDER = 2
HY_EMB = 33
HY_BANDS = (HY_EMB - 1) // 2
HY_SIN_FREQ = 1.0
HY_FAST_DECAY = 0.3
HY_SLOW_DECAY = 1.5
HY_TARGET = 1e-2
HY_MAX_DECAY = math.log(HY_TARGET) / HY_FAST_DECAY
HY_MIN_DECAY = math.log(HY_TARGET) / HY_SLOW_DECAY
N_EXPERTS = 8
EPS = 1e-6
Q_SCALE = DA_HEAD_DIM ** -0.5 * math.log2(math.e)

LANES = 128
MXU_COLS = 256
BF16_SUBLANES = 16
VMEM_LIMIT = 56 * 1024 * 1024

TOKEN_TILE = 512
FF_TILE = 512
CHAN_TILE = 256
MOE_CHUNK = 128
MOE_ROW_TILE = 1024


def _cparams(sem):
    return pltpu.CompilerParams(dimension_semantics=sem, vmem_limit_bytes=VMEM_LIMIT)


def _tile(n, target, align=LANES):
    if n <= target:
        return n
    t = target // align * align
    while n % t:
        t -= align
    return t


def _norm_mod(x, g, scale, shift):
    ms = jnp.mean(x * x, axis=-1, keepdims=True)
    return (x * lax.rsqrt(ms + EPS) * g) * (1.0 + scale) + shift


def _mod_kernel(c_ref, w_ref, b_ref, o_ref):
    c = c_ref[...]
    s = (c * jax.nn.sigmoid(c)).astype(BF16)
    o_ref[...] = jnp.dot(s, w_ref[...].astype(BF16), preferred_element_type=F32) + b_ref[...]


def _modulation(c_all, ada_w, ada_b):
    depth, d, n = ada_w.shape
    rows = c_all.shape[0]
    tn = _tile(n, 1024)
    return pl.pallas_call(
        _mod_kernel,
        grid=(depth, n // tn),
        in_specs=[
            pl.BlockSpec((rows, d), lambda l, j: (0, 0)),
            pl.BlockSpec((None, d, tn), lambda l, j: (l, 0, j)),
            pl.BlockSpec((None, 1, tn), lambda l, j: (l, 0, j)),
        ],
        out_specs=pl.BlockSpec((None, rows, tn), lambda l, j: (l, 0, j)),
        out_shape=jax.ShapeDtypeStruct((depth, rows, n), F32),
        compiler_params=_cparams(("parallel", "parallel")),
        name="adaln_modulation",
    )(c_all, ada_w, ada_b.reshape(depth, 1, n))


def _norm_matmul_kernel(x_ref, mod_ref, g_ref, w_ref, *rest, n_out, rope_cols, q_cols, n_lat, n_out_t):
    h = _norm_mod(x_ref[...], g_ref[...], mod_ref[1:2, :], mod_ref[0:1, :]).astype(BF16)
    rest = list(rest)
    if n_out_t:
        wt_ref = rest.pop(0)
        ot_ref = rest.pop()
    o_ref = rest.pop()
    if rope_cols:
        cos_ref, sa_ref, sb_ref = rest
        is_lat = pl.program_id(0) < n_lat
        cos = jnp.where(is_lat, cos_ref[...], 1.0)
        sa = jnp.where(is_lat, sa_ref[...], 0.0)
        sb = jnp.where(is_lat, sb_ref[...], 0.0)
    for c0 in range(0, n_out, MXU_COLS):
        acc = jnp.dot(h, w_ref[:, c0:c0 + MXU_COLS], preferred_element_type=F32)
        for c1 in range(c0, c0 + MXU_COLS, LANES):
            part = acc[:, c1 - c0:c1 - c0 + LANES]
            if c1 < rope_cols:
                part = (part * cos + pltpu.roll(part, LANES - ROPE_F, axis=1) * sa
                        + pltpu.roll(part, ROPE_F, axis=1) * sb)
                if c1 < q_cols:
                    part = part * Q_SCALE
            o_ref[:, c1:c1 + LANES] = part.astype(BF16)
    for c0 in range(0, n_out_t, MXU_COLS):
        acc = lax.dot_general(wt_ref[c0:c0 + MXU_COLS, :], h, (((1,), (1,)), ((), ())), preferred_element_type=F32)
        ot_ref[c0:c0 + MXU_COLS, :] = acc.astype(BF16)


def _norm_matmul(xs, mod, g, w, *, n_rows, rope=None, n_lat=0, w_t=None):
    nb, l, d = xs.shape
    n = w.shape[1]
    n_t = 0 if w_t is None else w_t.shape[0]
    tm = min(TOKEN_TILE, l)
    in_specs = [
        pl.BlockSpec((None, tm, d), lambda b, i: (b, i, 0)),
        pl.BlockSpec((None, 6, d), lambda b, i: (b, 0, 0)),
        pl.BlockSpec((1, d), lambda b, i: (0, 0)),
        pl.BlockSpec((d, n), lambda b, i: (0, 0)),
    ]
    args = [xs, mod, g.reshape(1, d), w]
    out_specs = [pl.BlockSpec((None, tm, n), lambda b, i: (b, i, 0))]
    out_shape = [jax.ShapeDtypeStruct((nb, l, n), BF16)]
    if n_t:
        in_specs.append(pl.BlockSpec((n_t, d), lambda b, i: (0, 0)))
        args.append(w_t)
        out_specs.append(pl.BlockSpec((None, n_t, tm), lambda b, i: (b, 0, i)))
        out_shape.append(jax.ShapeDtypeStruct((nb, n_t, l), BF16))
    rope_cols = q_cols = 0
    if rope is not None:
        rope_cols, q_cols = n, n // 2
        in_specs += [pl.BlockSpec((tm, LANES), lambda b, i: (i, 0))] * 3
        args += list(rope)
    out = pl.pallas_call(
        functools.partial(_norm_matmul_kernel, n_out=n, rope_cols=rope_cols, q_cols=q_cols, n_lat=n_lat,
                          n_out_t=n_t),
        grid=(n_rows, l // tm),
        in_specs=in_specs,
        out_specs=out_specs,
        out_shape=out_shape,
        compiler_params=_cparams(("parallel", "parallel")),
        name="norm_matmul",
    )(*args)
    return out if n_t else out[0]


def _rope_tables(l):
    t = jnp.arange(l, dtype=jnp.int32)
    row = (t // GRID_W).astype(F32)
    col = (t % GRID_W).astype(F32)
    freqs = ROPE_THETA ** (-jnp.arange(ROPE_F, dtype=F32) / ROPE_F)
    lane = jnp.arange(LANES, dtype=jnp.int32)
    axis = (lane // (2 * ROPE_F)) % 2
    f = freqs[lane % ROPE_F]
    pos = jnp.where(axis[None, :] == 0, row[:, None], col[:, None])
    ang = pos * f[None, :]
    cos, sin = jnp.cos(ang), jnp.sin(ang)
    lower = (lane % (2 * ROPE_F)) < ROPE_F
    sa = jnp.where(lower[None, :], -sin, 0.0)
    sb = jnp.where(lower[None, :], 0.0, sin)
    return cos, sa, sb


def _attn_kernel(lqk_ref, g_ref, q_ref, *refs, nseg, lam_init):
    kv, o_ref = refs[:2 * nseg], refs[2 * nseg]
    q = q_ref[...]
    tq, hd = q.shape
    lq = lqk_ref[...]
    lam = (jnp.exp(jnp.sum(lq[0:1] * lq[1:2], axis=-1, keepdims=True))
           - jnp.exp(jnp.sum(lq[2:3] * lq[3:4], axis=-1, keepdims=True)) + lam_init)
    lane = lax.broadcasted_iota(jnp.int32, q.shape, 1)
    zero = jnp.zeros_like(q)
    q2 = jnp.concatenate([jnp.where(lane < DA_HEAD_DIM, q, zero), jnp.where(lane < DA_HEAD_DIM, zero, q)], axis=0)
    nt = (((1,), (1,)), ((), ()))
    s = [lax.dot_general(kv[2 * i][...], q2, nt, preferred_element_type=F32) for i in range(nseg)]
    m = functools.reduce(jnp.maximum, [jnp.max(si, axis=0, keepdims=True) for si in s])
    acc = None
    for i in range(nseg):
        vt = kv[2 * i + 1][...]
        vt_ones = jnp.concatenate([vt, jnp.ones((BF16_SUBLANES, vt.shape[1]), BF16)], axis=0)
        part = jnp.dot(vt_ones, jnp.exp2(s[i] - m).astype(BF16), preferred_element_type=F32)
        acc = part if acc is None else acc + part
    w0 = 1.0 / acc[hd:hd + 1, :tq]
    w1 = lam / acc[hd:hd + 1, tq:]
    o = (acc[:hd, :tq] * w0 - acc[:hd, tq:] * w1).T
    ms = jnp.mean(o * o, axis=-1, keepdims=True)
    o_ref[...] = ((o * lax.rsqrt(ms + EPS) * g_ref[...]) * (1.0 - lam_init)).astype(BF16)


def _attn_kernel_aliased(lqk_ref, g_ref, q_ref, k_ref, v_ref, prev_ref, o_ref, *, nseg, lam_init):
    del prev_ref
    _attn_kernel(lqk_ref, g_ref, q_ref, k_ref, v_ref, o_ref, nseg=nseg, lam_init=lam_init)


def _attention(qk, vt, lqk, subln_g, *, lam_init, batch, ctx_len, with_ctx):
    nb, d, l = vt.shape
    hd = 2 * DA_HEAD_DIM
    heads = d // hd
    tq = min(TOKEN_TILE, l)
    per_row = l // ctx_len
    g2 = subln_g.reshape(1, hd)

    def ctx_row(b):
        return batch + b // per_row

    small = [pl.BlockSpec((4, DA_HEAD_DIM), lambda b, h, i: (0, 0)),
             pl.BlockSpec((1, hd), lambda b, h, i: (0, 0))]
    lat = pl.pallas_call(
        functools.partial(_attn_kernel, nseg=2, lam_init=lam_init),
        grid=(batch, heads, l // tq),
        in_specs=small + [
            pl.BlockSpec((None, tq, hd), lambda b, h, i: (b, i, h)),
            pl.BlockSpec((None, ctx_len, hd), lambda b, h, i: (ctx_row(b), b % per_row, heads + h)),
            pl.BlockSpec((None, hd, ctx_len), lambda b, h, i: (ctx_row(b), h, b % per_row)),
            pl.BlockSpec((None, l, hd), lambda b, h, i: (b, 0, heads + h)),
            pl.BlockSpec((None, hd, l), lambda b, h, i: (b, h, 0)),
        ],
        out_specs=pl.BlockSpec((None, tq, hd), lambda b, h, i: (b, i, h)),
        out_shape=jax.ShapeDtypeStruct((nb, l, d), BF16),
        compiler_params=_cparams(("parallel", "parallel", "parallel")),
        name="diff_attention_latent",
    )(lqk, g2, qk, qk, vt, qk, vt)
    if not with_ctx:
        return lat
    seqs = nb * per_row
    qk_c = qk.reshape(seqs, ctx_len, 2 * d)
    off = batch * per_row
    small2 = [pl.BlockSpec((4, DA_HEAD_DIM), lambda b, h: (0, 0)),
              pl.BlockSpec((1, hd), lambda b, h: (0, 0))]
    out = pl.pallas_call(
        functools.partial(_attn_kernel_aliased, nseg=1, lam_init=lam_init),
        grid=(batch, heads),
        in_specs=small2 + [
            pl.BlockSpec((None, ctx_len, hd), lambda b, h: (off + b, 0, h)),
            pl.BlockSpec((None, ctx_len, hd), lambda b, h: (off + b, 0, heads + h)),
            pl.BlockSpec((None, hd, ctx_len), lambda b, h: (ctx_row(b), h, b % per_row)),
            pl.BlockSpec(memory_space=pl.ANY),
        ],
        out_specs=pl.BlockSpec((None, ctx_len, hd), lambda b, h: (off + b, 0, h)),
        out_shape=jax.ShapeDtypeStruct((seqs, ctx_len, d), BF16),
        input_output_aliases={5: 0},
        compiler_params=_cparams(("parallel", "parallel")),
        name="diff_attention_context",
    )(lqk, g2, qk_c, qk_c, vt, lat.reshape(seqs, ctx_len, d))
    return out.reshape(nb, l, d)


def _proj_residual_kernel(a_ref, w_ref, *rest, gate_row, has_bias):
    if has_bias:
        b_ref, mod_ref, x_ref, o_ref = rest
    else:
        mod_ref, x_ref, o_ref = rest
    y = jnp.dot(a_ref[...], w_ref[...], preferred_element_type=F32)
    if has_bias:
        y = y + b_ref[...]
    o_ref[...] = x_ref[...] + mod_ref[gate_row:gate_row + 1, :] * y


def _proj_residual(a, w, mod, xs, *, n_rows, gate_row, bias=None):
    nb, l, d = xs.shape
    k = a.shape[-1]
    tm = min(TOKEN_TILE, l)
    in_specs = [pl.BlockSpec((None, tm, k), lambda b, i: (b, i, 0)),
                pl.BlockSpec((k, d), lambda b, i: (0, 0))]
    args = [a, w]
    if bias is not None:
        in_specs.append(pl.BlockSpec((1, d), lambda b, i: (0, 0)))
        args.append(bias.reshape(1, d))
    in_specs += [pl.BlockSpec((None, 6, d), lambda b, i: (b, 0, 0)),
                 pl.BlockSpec((None, tm, d), lambda b, i: (b, i, 0))]
    args += [mod, xs]
    return pl.pallas_call(
        functools.partial(_proj_residual_kernel, gate_row=gate_row, has_bias=bias is not None),
        grid=(n_rows, l // tm),
        in_specs=in_specs,
        out_specs=pl.BlockSpec((None, tm, d), lambda b, i: (b, i, 0)),
        out_shape=jax.ShapeDtypeStruct((nb, l, d), F32),
        input_output_aliases={len(args) - 1: 0},
        compiler_params=_cparams(("parallel", "parallel")),
        name="proj_residual",
    )(*args)


def _swiglu_partial(h, w1_ref, w3_ref, w2_ref):
    a = jnp.dot(h, w1_ref[...], preferred_element_type=F32)
    b = jnp.dot(h, w3_ref[...], preferred_element_type=F32)
    g = ((a * jax.nn.sigmoid(a)) * b).astype(BF16)
    return jnp.dot(g, w2_ref[...], preferred_element_type=F32)


def _ffn_kernel(x_ref, mod_ref, g_ref, w1_ref, w3_ref, w2_ref, o_ref, h_ref, acc_ref):
    f = pl.program_id(2)

    @pl.when(f == 0)
    def _():
        h_ref[...] = _norm_mod(x_ref[...], g_ref[...], mod_ref[4:5, :], mod_ref[3:4, :]).astype(BF16)
        acc_ref[...] = jnp.zeros_like(acc_ref)

    acc_ref[...] += _swiglu_partial(h_ref[...], w1_ref, w3_ref, w2_ref)

    @pl.when(f == pl.num_programs(2) - 1)
    def _():
        o_ref[...] = x_ref[...] + mod_ref[5:6, :] * acc_ref[...]


def _ffn(xs, mod, g, w1, w3, w2, *, n_rows):
    nb, l, d = xs.shape
    ff = w1.shape[1]
    tm = min(2 * TOKEN_TILE, l)
    tf = _tile(ff, FF_TILE)
    return pl.pallas_call(
        _ffn_kernel,
        grid=(n_rows, l // tm, ff // tf),
        in_specs=[
            pl.BlockSpec((None, tm, d), lambda b, i, f: (b, i, 0)),
            pl.BlockSpec((None, 6, d), lambda b, i, f: (b, 0, 0)),
            pl.BlockSpec((1, d), lambda b, i, f: (0, 0)),
            pl.BlockSpec((d, tf), lambda b, i, f: (0, f)),
            pl.BlockSpec((d, tf), lambda b, i, f: (0, f)),
            pl.BlockSpec((tf, d), lambda b, i, f: (f, 0)),
        ],
        out_specs=pl.BlockSpec((None, tm, d), lambda b, i, f: (b, i, 0)),
        out_shape=jax.ShapeDtypeStruct((nb, l, d), F32),
        scratch_shapes=[pltpu.VMEM((tm, d), BF16), pltpu.VMEM((tm, d), F32)],
        input_output_aliases={0: 0},
        compiler_params=_cparams(("parallel", "parallel", "arbitrary")),
        name="dense_swiglu",
    )(xs, mod, g.reshape(1, d), w1, w3, w2)


def _dft_cos_sin(n_out, n_in, num, den, scale):
    k = jnp.arange(n_out, dtype=jnp.int32)[:, None]
    s = jnp.arange(n_in, dtype=jnp.int32)[None, :]
    ang = (num(k, s) % den).astype(F32) * (2.0 * math.pi / den)
    return (jnp.cos(ang) * scale).astype(BF16), (jnp.sin(ang) * scale).astype(BF16)


def _fnet_chan_kernel(x_ref, mod_ref, g_ref, cs_ref, a_ref, b_ref, *, gd):
    h = _norm_mod(x_ref[...], g_ref[...], mod_ref[1:2, :], mod_ref[0:1, :]).astype(BF16)
    for c0 in range(0, h.shape[1], gd):
        r = jnp.dot(h[:, c0:c0 + gd], cs_ref[...], preferred_element_type=F32)
        a_ref[:, c0:c0 + gd] = r[:, :gd].astype(BF16)
        b_ref[:, c0:c0 + gd] = r[:, gd:].astype(BF16)


def _fnet_seq_kernel(c_ref, s_ref, a_ref, b_ref, w_ref, bias_ref, mod_ref, x_ref, o_ref):
    f = (jnp.dot(c_ref[...], a_ref[...], preferred_element_type=F32)
         - jnp.dot(s_ref[...], b_ref[...], preferred_element_type=F32)).astype(BF16)
    y = jnp.dot(f, w_ref[...], preferred_element_type=F32) + bias_ref[...]
    o_ref[...] = x_ref[...] + mod_ref[2:3, :] * y


def _fnet_seq(a, b, xs, mod, w_o, b_o, *, seq, n_seq, seq_off, mod_row):
    nb, l, d = xs.shape
    seqs = nb * l // seq
    tm = min(TOKEN_TILE, seq)
    cm, sm = _dft_cos_sin(seq, seq, lambda k, s: k * s, seq, seq ** -0.5)
    view = lambda t: t.reshape(seqs, seq, d)
    out = pl.pallas_call(
        _fnet_seq_kernel,
        grid=(n_seq, seq // tm),
        in_specs=[
            pl.BlockSpec((tm, seq), lambda b, i: (i, 0)),
            pl.BlockSpec((tm, seq), lambda b, i: (i, 0)),
            pl.BlockSpec((None, seq, d), lambda b, i: (seq_off + b, 0, 0)),
            pl.BlockSpec((None, seq, d), lambda b, i: (seq_off + b, 0, 0)),
            pl.BlockSpec((d, d), lambda b, i: (0, 0)),
            pl.BlockSpec((1, d), lambda b, i: (0, 0)),
            pl.BlockSpec((None, 6, d), lambda b, i: (mod_row(b), 0, 0)),
            pl.BlockSpec((None, tm, d), lambda b, i: (seq_off + b, i, 0)),
        ],
        out_specs=pl.BlockSpec((None, tm, d), lambda b, i: (seq_off + b, i, 0)),
        out_shape=jax.ShapeDtypeStruct((seqs, seq, d), F32),
        input_output_aliases={7: 0},
        compiler_params=_cparams(("parallel", "parallel")),
        name="fourier_seq_proj_residual",
    )(cm, sm, view(a), view(b), w_o, b_o.reshape(1, d), mod, view(xs))
    return out.reshape(nb, l, d)


def _fourier_layer(xs, mod, g, w_o, b_o, *, batch, ctx_len, with_ctx):
    nb, l, d = xs.shape
    gd = d // FN_GROUPS
    tm = min(TOKEN_TILE, l)
    n_rows = nb if with_ctx else batch
    cd, sd = _dft_cos_sin(gd, gd, lambda k, s: k * s, gd, gd ** -0.5)
    a, b = pl.pallas_call(
        functools.partial(_fnet_chan_kernel, gd=gd),
        grid=(n_rows, l // tm),
        in_specs=[
            pl.BlockSpec((None, tm, d), lambda r, i: (r, i, 0)),
            pl.BlockSpec((None, 6, d), lambda r, i: (r, 0, 0)),
            pl.BlockSpec((1, d), lambda r, i: (0, 0)),
            pl.BlockSpec((gd, 2 * gd), lambda r, i: (0, 0)),
        ],
        out_specs=[pl.BlockSpec((None, tm, d), lambda r, i: (r, i, 0))] * 2,
        out_shape=[jax.ShapeDtypeStruct((nb, l, d), BF16)] * 2,
        compiler_params=_cparams(("parallel", "parallel")),
        name="fourier_channel_dft",
    )(xs, mod, g.reshape(1, d), jnp.concatenate([cd, sd], axis=1))
    w = w_o.astype(BF16)
    xs = _fnet_seq(a, b, xs, mod, w, b_o, seq=l, n_seq=batch, seq_off=0, mod_row=lambda r: r)
    if with_ctx:
        xs = _fnet_seq(a, b, xs, mod, w, b_o, seq=ctx_len, n_seq=batch, seq_off=batch * (l // ctx_len),
                       mod_row=lambda r: batch)
    return xs


def _hyena_filter_kernel(emb_ref, w1_ref, b1_ref, w2_ref, b2_ref, w3f_ref, w3b_ref, dec_ref, c0_ref, s0_ref,
                         kr_ref, ki_ref, *, scale):
    dot = functools.partial(jnp.dot, precision=HIGHEST, preferred_element_type=F32)
    hdn = jnp.sin(HY_SIN_FREQ * (dot(emb_ref[...], w1_ref[...]) + b1_ref[...]))
    hdn = jnp.sin(HY_SIN_FREQ * (dot(hdn, w2_ref[...]) + b2_ref[...]))
    dec = dec_ref[...] * scale
    fwd = dot(hdn, w3f_ref[...]) * dec
    bwd = dot(hdn, w3b_ref[...]) * dec
    row = lax.broadcasted_iota(jnp.int32, bwd.shape, 0)
    bwd = jnp.where(row == 0, 0.0, bwd)
    kr_ref[...] = jnp.dot(c0_ref[...], (fwd + bwd).astype(BF16), preferred_element_type=F32).astype(BF16)
    ki_ref[...] = jnp.dot(s0_ref[...], (bwd - fwd).astype(BF16), preferred_element_type=F32).astype(BF16)


def _hyena_spectra(seq, d, f_w1, f_b1, f_w2, f_b2, f_w3):
    n2 = 2 * seq
    hid = f_w1.shape[1]
    pad = LANES
    t01 = jnp.linspace(0.0, 1.0, seq, dtype=F32)[:, None]
    wv = 2.0 * math.pi * jnp.arange(seq, dtype=F32) / seq
    bands = jnp.linspace(1e-4, HY_BANDS - 1, HY_BANDS, dtype=F32)
    fw = wv[:, None] * bands[None, :]
    emb = jnp.concatenate([t01, jnp.cos(fw), -jnp.sin(fw)], axis=-1)
    emb = jnp.pad(emb, ((0, 0), (0, pad - HY_EMB)))
    deltas = jnp.abs(jnp.linspace(HY_MIN_DECAY, HY_MAX_DECAY, d, dtype=F32))
    decay = jnp.exp(-t01 * deltas[None, :])
    w1 = jnp.pad(f_w1, ((0, pad - HY_EMB), (0, pad - hid)))
    b1 = jnp.pad(f_b1, (0, pad - hid)).reshape(1, pad)
    w2 = jnp.pad(f_w2, ((0, pad - hid), (0, pad - hid)))
    b2 = jnp.pad(f_b2, (0, pad - hid)).reshape(1, pad)
    w3 = jnp.pad(f_w3, ((0, pad - hid), (0, 0)))
    c0, s0 = _dft_cos_sin(seq, seq, lambda k, m: (2 * k + 1) * m, 2 * n2, 1.0)
    tc = min(CHAN_TILE, d)
    tk = min(TOKEN_TILE, seq)
    nct = d // tc
    full = lambda shape: pl.BlockSpec(shape, lambda n, j, k: (0,) * len(shape))
    kr, ki = pl.pallas_call(
        functools.partial(_hyena_filter_kernel, scale=2.0 / n2),
        grid=(HY_ORDER, nct, seq // tk),
        in_specs=[
            full((seq, pad)), full((pad, pad)), full((1, pad)), full((pad, pad)), full((1, pad)),
            pl.BlockSpec((pad, tc), lambda n, j, k: (0, (2 * n) * nct + j)),
            pl.BlockSpec((pad, tc), lambda n, j, k: (0, (2 * n + 1) * nct + j)),
            pl.BlockSpec((seq, tc), lambda n, j, k: (0, j)),
            pl.BlockSpec((tk, seq), lambda n, j, k: (k, 0)),
            pl.BlockSpec((tk, seq), lambda n, j, k: (k, 0)),
        ],
        out_specs=[pl.BlockSpec((None, tk, tc), lambda n, j, k: (n, k, j))] * 2,
        out_shape=[jax.ShapeDtypeStruct((HY_ORDER, seq, d), BF16)] * 2,
        compiler_params=_cparams(("parallel", "parallel", "parallel")),
        name="hyena_filter_spectrum",
    )(emb, w1, b1, w2, b2, w3, w3, decay, c0, s0)
    return jnp.stack([kr, ki], axis=1).reshape(2 * HY_ORDER, seq, d)


def _hyena_conv_kernel(x1_ref, x2_ref, v_ref, cw1_ref, cw2_ref, cwv_ref, cb1_ref, cb2_ref, cbv_ref,
                       k_ref, fb_ref, cs_ref, ss_ref, *rest):
    o_ref = rest[-1]
    seq = v_ref.shape[0]

    def short_conv(u_ref, w_ref, b_ref):
        u = u_ref[...].astype(F32)
        row = lax.broadcasted_iota(jnp.int32, u.shape, 0)
        prev = jnp.where(row == 0, 0.0, pltpu.roll(u, 1, axis=0))
        nxt = jnp.where(row == seq - 1, 0.0, pltpu.roll(u, seq - 1, axis=0))
        return prev * w_ref[0:1, :] + u * w_ref[1:2, :] + nxt * w_ref[2:3, :] + b_ref[...]

    gates = (short_conv(x1_ref, cw1_ref, cb1_ref), short_conv(x2_ref, cw2_ref, cb2_ref))
    z = short_conv(v_ref, cwv_ref, cbv_ref)
    for n in range(HY_ORDER):
        zb = z.astype(BF16)
        p = jnp.dot(cs_ref[...], zb, preferred_element_type=F32)
        q = jnp.dot(ss_ref[...], zb, preferred_element_type=F32)
        kr = k_ref[2 * n].astype(F32)
        ki = k_ref[2 * n + 1].astype(F32)
        yr = (p * kr + q * ki).astype(BF16)
        yi = (p * ki - q * kr).astype(BF16)
        y = (jnp.dot(cs_ref[...], yr, preferred_element_type=F32)
             - jnp.dot(ss_ref[...], yi, preferred_element_type=F32))
        z = gates[n] * (y + fb_ref[n:n + 1, :] * z)
    o_ref[...] = z.astype(BF16)


def _hyena_conv(u, spectra, conv_w, conv_b, f_bias, prev, *, seq, n_seq, seq_off):
    nb, l, d3 = u.shape
    d = d3 // 3
    seqs = nb * l // seq
    tc = min(CHAN_TILE, d)
    nct = d // tc
    cs, ss = _dft_cos_sin(seq, seq, lambda k, s: (2 * k + 1) * (2 * s + 1), 8 * seq, 1.0)
    uv = u.reshape(seqs, seq, d3)
    cb = conv_b.reshape(1, d3)
    u_spec = lambda part: pl.BlockSpec((None, seq, tc), lambda j, r: (seq_off + r, 0, part * nct + j))
    w_spec = lambda part: pl.BlockSpec((3, tc), lambda j, r: (0, part * nct + j))
    b_spec = lambda part: pl.BlockSpec((1, tc), lambda j, r: (0, part * nct + j))
    in_specs = [u_spec(0), u_spec(1), u_spec(2), w_spec(0), w_spec(1), w_spec(2), b_spec(0), b_spec(1), b_spec(2),
                pl.BlockSpec((2 * HY_ORDER, seq, tc), lambda j, r: (0, 0, j), pipeline_mode=pl.Buffered(1)),
                pl.BlockSpec((HY_ORDER, tc), lambda j, r: (0, j)),
                pl.BlockSpec((seq, seq), lambda j, r: (0, 0), pipeline_mode=pl.Buffered(1)),
                pl.BlockSpec((seq, seq), lambda j, r: (0, 0), pipeline_mode=pl.Buffered(1))]
    args = [uv, uv, uv, conv_w, conv_w, conv_w, cb, cb, cb, spectra, f_bias, cs, ss]
    aliases = {}
    if prev is not None:
        in_specs.append(pl.BlockSpec(memory_space=pl.ANY))
        args.append(prev.reshape(seqs, seq, d))
        aliases = {len(args) - 1: 0}
    out = pl.pallas_call(
        _hyena_conv_kernel,
        grid=(nct, n_seq),
        in_specs=in_specs,
        out_specs=pl.BlockSpec((None, seq, tc), lambda j, r: (seq_off + r, 0, j)),
        out_shape=jax.ShapeDtypeStruct((seqs, seq, d), BF16),
        input_output_aliases=aliases,
        compiler_params=_cparams(("parallel", "parallel")),
        name="hyena_long_conv",
    )(*args)
    return out.reshape(nb, l, d)


def _round_up(v, m):
    return (v + m - 1) // m * m


def _run_pieces(n_rows):
    top = (n_rows // BF16_SUBLANES).bit_length() - 1
    return [(k, BF16_SUBLANES << k) for k in range(top, -1, -1)]


def _piece_offset(u, k):
    return ((u >> (k + 1)) << (k + 1)) * BF16_SUBLANES


def _tile_offsets(counts):
    units = [(c + BF16_SUBLANES - 1) // BF16_SUBLANES for c in counts]
    offs, acc = [], jnp.int32(0)
    for u in units:
        offs.append(acc)
        acc = acc + u * BF16_SUBLANES
    return units, offs


def _moe_route_kernel(x_ref, mod_ref, g_ref, rt_ref, xs_hbm, info_ref, base_ref, cnt_ref, used_ref,
                      run_ref, hist_ref, stage_ref, zero_ref, sem, zsem, *, tiles_per_row, n_tiles, cap_rows,
                      row_tile):
    tm, d = x_ref.shape
    block_rows = stage_ref.shape[1]
    i = pl.program_id(0) * tiles_per_row + pl.program_id(1)
    slot = i % 2

    @pl.when(i == 0)
    def _():
        for e in range(N_EXPERTS):
            run_ref[e] = 0

    xn = _norm_mod(x_ref[...], g_ref[...], mod_ref[4:5, :], mod_ref[3:4, :])
    hb = xn.astype(BF16)
    lt = lax.dot_general(rt_ref[...], xn, (((1,), (1,)), ((), ())), precision=HIGHEST, preferred_element_type=F32)
    eio = lax.broadcasted_iota(jnp.int32, lt.shape, 0)
    m1 = jnp.max(lt, axis=0, keepdims=True)
    i1 = jnp.min(jnp.where(lt == m1, eio, N_EXPERTS), axis=0, keepdims=True)
    mask1 = eio == i1
    l2 = jnp.where(mask1, -jnp.inf, lt)
    m2 = jnp.max(l2, axis=0, keepdims=True)
    i2 = jnp.min(jnp.where(l2 == m2, eio, N_EXPERTS), axis=0, keepdims=True)
    mask2 = eio == i2
    ee = jnp.exp(m2 - m1)
    gate1 = 1.0 / (1.0 + ee)
    gate2 = ee / (1.0 + ee)
    sel = jnp.where(mask1 | mask2, 1.0, 0.0)
    t0 = lax.broadcasted_iota(jnp.int32, (tm, tm), 0)
    t1 = lax.broadcasted_iota(jnp.int32, (tm, tm), 1)
    before = jnp.where(t0 < t1, 1.0, 0.0).astype(BF16)
    rank = jnp.dot(sel.astype(BF16), before, preferred_element_type=F32)

    counts = [jnp.sum(sel[e:e + 1, :]).astype(jnp.int32) for e in range(N_EXPERTS)]
    units, offs = _tile_offsets(counts)
    pos1 = jnp.zeros((1, tm), F32)
    pos2 = jnp.zeros((1, tm), F32)
    for e in range(N_EXPERTS):
        pos_e = rank[e:e + 1, :] + offs[e].astype(F32)
        pos1 = pos1 + jnp.where(mask1[e:e + 1, :], pos_e, 0.0)
        pos2 = pos2 + jnp.where(mask2[e:e + 1, :], pos_e, 0.0)
    packed = jnp.concatenate([pos1, pos2, gate1, gate2, jnp.zeros((LANES - 4, tm), F32)], axis=0)
    info_ref[...] = packed.T

    r = lax.broadcasted_iota(jnp.int32, (block_rows, tm), 0).astype(F32)
    onehot = jnp.where((r == pos1) | (r == pos2), 1.0, 0.0).astype(BF16)
    stage_ref[slot] = jnp.dot(onehot, hb, preferred_element_type=F32).astype(BF16)

    def run_copies(sl, e, base, off, u):
        out = []
        for k, rows in _run_pieces(tm):
            po = _piece_offset(u, k)
            src = stage_ref.at[sl, pl.ds(pl.multiple_of(off + po, BF16_SUBLANES), rows)]
            dst = xs_hbm.at[pl.ds(pl.multiple_of(e * cap_rows + base + po, BF16_SUBLANES), rows)]
            out.append(((u >> k) & 1, pltpu.make_async_copy(src, dst, sem.at[sl, e])))
        return out

    for e in range(N_EXPERTS):
        base = run_ref[e]
        base_ref[i * N_EXPERTS + e] = base
        cnt_ref[i * N_EXPERTS + e] = counts[e]
        hist_ref[slot, e] = base
        hist_ref[slot, N_EXPERTS + e] = offs[e]
        hist_ref[slot, 2 * N_EXPERTS + e] = units[e]
        for bit, cp in run_copies(slot, e, base, offs[e], units[e]):
            @pl.when(bit == 1)
            def _():
                cp.start()
        run_ref[e] = base + units[e] * BF16_SUBLANES

    def wait_slot(sl):
        for e in range(N_EXPERTS):
            copies = run_copies(sl, e, hist_ref[sl, e], hist_ref[sl, N_EXPERTS + e], hist_ref[sl, 2 * N_EXPERTS + e])
            for bit, cp in copies:
                @pl.when(bit == 1)
                def _():
                    cp.wait()

    @pl.when(i > 0)
    def _():
        wait_slot(1 - slot)

    @pl.when(i == n_tiles - 1)
    def _():
        wait_slot(slot)
        zero_ref[...] = jnp.zeros_like(zero_ref)
        for e in range(N_EXPERTS):
            used = run_ref[e]
            used_ref[e] = used
            gap = (_round_up(used, row_tile) - used) // BF16_SUBLANES
            small = [(k, BF16_SUBLANES << k) for k in range(2, -1, -1)]
            for k, rows in small:
                @pl.when(((gap >> k) & 1) == 1)
                def _():
                    lo = (((gap & 7) >> (k + 1)) << (k + 1)) * BF16_SUBLANES
                    dst = xs_hbm.at[pl.ds(pl.multiple_of(e * cap_rows + used + lo, BF16_SUBLANES), rows)]
                    cp = pltpu.make_async_copy(zero_ref.at[pl.ds(0, rows)], dst, zsem)
                    cp.start()
                    cp.wait()
            for jz in range(row_tile // MOE_CHUNK):
                @pl.when(jz < (gap >> 3))
                def _():
                    lo = (gap & 7) * BF16_SUBLANES + jz * MOE_CHUNK
                    dst = xs_hbm.at[pl.ds(pl.multiple_of(e * cap_rows + used + lo, BF16_SUBLANES), MOE_CHUNK)]
                    cp = pltpu.make_async_copy(zero_ref, dst, zsem)
                    cp.start()
                    cp.wait()


def _moe_block_rows(tm):
    return _round_up(2 * tm + N_EXPERTS * (BF16_SUBLANES - 1), LANES)


def _moe_route(xs, mod, g, router, *, n_rows, cap_rows, row_tile):
    nb, l, d = xs.shape
    tm = min(TOKEN_TILE, l)
    tpr = l // tm
    n_tiles = n_rows * tpr
    smem = pl.BlockSpec(memory_space=pltpu.SMEM)
    return pl.pallas_call(
        functools.partial(_moe_route_kernel, tiles_per_row=tpr, n_tiles=n_tiles, cap_rows=cap_rows,
                          row_tile=row_tile),
        grid=(n_rows, tpr),
        in_specs=[
            pl.BlockSpec((None, tm, d), lambda b, i: (b, i, 0)),
            pl.BlockSpec((None, 6, d), lambda b, i: (b, 0, 0)),
            pl.BlockSpec((1, d), lambda b, i: (0, 0)),
            pl.BlockSpec((N_EXPERTS, d), lambda b, i: (0, 0)),
        ],
        out_specs=[
            pl.BlockSpec(memory_space=pl.ANY),
            pl.BlockSpec((None, tm, LANES), lambda b, i: (b * tpr + i, 0, 0)),
            smem, smem, smem,
        ],
        out_shape=[
            jax.ShapeDtypeStruct((N_EXPERTS * cap_rows, d), BF16),
            jax.ShapeDtypeStruct((n_tiles, tm, LANES), F32),
            jax.ShapeDtypeStruct((n_tiles * N_EXPERTS,), jnp.int32),
            jax.ShapeDtypeStruct((n_tiles * N_EXPERTS,), jnp.int32),
            jax.ShapeDtypeStruct((N_EXPERTS,), jnp.int32),
        ],
        scratch_shapes=[
            pltpu.SMEM((N_EXPERTS,), jnp.int32),
            pltpu.SMEM((2, 3 * N_EXPERTS), jnp.int32),
            pltpu.VMEM((2, _moe_block_rows(tm), d), BF16),
            pltpu.VMEM((MOE_CHUNK, d), BF16),
            pltpu.SemaphoreType.DMA((2, N_EXPERTS)),
            pltpu.SemaphoreType.DMA(()),
        ],
        compiler_params=_cparams(("arbitrary", "arbitrary")),
        name="moe_route_dispatch",
    )(xs, mod, g.reshape(1, d), router.T)


def _moe_expert_kernel(te_ref, tr_ref, ta_ref, x_ref, w1_ref, w3_ref, w2_ref, o_ref, acc_ref):
    del te_ref, tr_ref
    f = pl.program_id(1)

    @pl.when(ta_ref[pl.program_id(0)] == 1)
    def _():
        @pl.when(f == 0)
        def _():
            acc_ref[...] = jnp.zeros_like(acc_ref)

        acc_ref[...] += _swiglu_partial(x_ref[...], w1_ref, w3_ref, w2_ref)

        @pl.when(f == pl.num_programs(1) - 1)
        def _():
            o_ref[...] = acc_ref[...].astype(BF16)


def _moe_experts(xsorted, tile_expert, tile_row, tile_active, w1, w3, w2, *, row_tile):
    rows, d = xsorted.shape
    ff = w1.shape[2]
    tf = _tile(ff, FF_TILE)
    nf = ff // tf
    n_tiles = tile_expert.shape[0]

    def fidx(g, f, ta):
        return jnp.where(ta[g] == 1, f, nf - 1)

    grid_spec = pltpu.PrefetchScalarGridSpec(
        num_scalar_prefetch=3,
        grid=(n_tiles, nf),
        in_specs=[
            pl.BlockSpec((row_tile, d), lambda g, f, te, tr, ta: (tr[g], 0)),
            pl.BlockSpec((None, d, tf), lambda g, f, te, tr, ta: (te[g], 0, fidx(g, f, ta))),
            pl.BlockSpec((None, d, tf), lambda g, f, te, tr, ta: (te[g], 0, fidx(g, f, ta))),
            pl.BlockSpec((None, tf, d), lambda g, f, te, tr, ta: (te[g], fidx(g, f, ta), 0)),
        ],
        out_specs=pl.BlockSpec((row_tile, d), lambda g, f, te, tr, ta: (tr[g], 0)),
        scratch_shapes=[pltpu.VMEM((row_tile, d), F32)],
    )
    return pl.pallas_call(
        _moe_expert_kernel,
        grid_spec=grid_spec,
        out_shape=jax.ShapeDtypeStruct((rows, d), BF16),
        compiler_params=_cparams(("arbitrary", "arbitrary")),
        name="moe_expert_swiglu",
    )(tile_expert, tile_row, tile_active, xsorted, w1, w3, w2)


def _moe_combine_kernel(base_ref, cnt_ref, x_ref, mod_ref, info_ref, y_hbm, *rest, tiles_per_row, n_tiles,
                        cap_rows, final_norm):
    if final_norm:
        fg_ref, o_ref, buf_ref, sem = rest
    else:
        o_ref, buf_ref, sem = rest
    tm, d = x_ref.shape
    block_rows = buf_ref.shape[1]
    i = pl.program_id(0) * tiles_per_row + pl.program_id(1)
    slot = i % 2

    def run_copies(tile, sl):
        counts = [cnt_ref[tile * N_EXPERTS + e] for e in range(N_EXPERTS)]
        units, offs = _tile_offsets(counts)
        out = []
        for e in range(N_EXPERTS):
            base = base_ref[tile * N_EXPERTS + e]
            for k, rows in _run_pieces(tm):
                po = _piece_offset(units[e], k)
                src = y_hbm.at[pl.ds(pl.multiple_of(e * cap_rows + base + po, BF16_SUBLANES), rows)]
                dst = buf_ref.at[sl, pl.ds(pl.multiple_of(offs[e] + po, BF16_SUBLANES), rows)]
                out.append(((units[e] >> k) & 1, pltpu.make_async_copy(src, dst, sem.at[sl, e])))
        return out

    def start_tile(tile, sl):
        for bit, cp in run_copies(tile, sl):
            @pl.when(bit == 1)
            def _():
                cp.start()

    @pl.when(i == 0)
    def _():
        buf_ref[...] = jnp.zeros_like(buf_ref)
        start_tile(0, 0)

    @pl.when(i + 1 < n_tiles)
    def _():
        start_tile(i + 1, 1 - slot)

    for bit, cp in run_copies(i, slot):
        @pl.when(bit == 1)
        def _():
            cp.wait()

    info = info_ref[...]
    rows = buf_ref[slot]
    r = lax.broadcasted_iota(jnp.int32, (tm, block_rows), 1).astype(F32)
    first = jnp.where(r == info[:, 0:1], 1.0, 0.0).astype(BF16)
    second = jnp.where(r == info[:, 1:2], 1.0, 0.0).astype(BF16)
    y = (info[:, 2:3] * jnp.dot(first, rows, preferred_element_type=F32)
         + info[:, 3:4] * jnp.dot(second, rows, preferred_element_type=F32))
    out = x_ref[...] + mod_ref[5:6, :] * y
    if final_norm:
        ms = jnp.mean(out * out, axis=-1, keepdims=True)
        out = out * lax.rsqrt(ms + EPS) * fg_ref[...]
    o_ref[...] = out


def _moe_combine(xs, mod, info, base, cnt, ysorted, *, n_rows, cap_rows, final_g=None):
    nb, l, d = xs.shape
    tm = min(TOKEN_TILE, l)
    tpr = l // tm
    final_norm = final_g is not None
    in_specs = [
        pl.BlockSpec((None, tm, d), lambda b, i, bs, ct: (b, i, 0)),
        pl.BlockSpec((None, 6, d), lambda b, i, bs, ct: (b, 0, 0)),
        pl.BlockSpec((None, tm, LANES), lambda b, i, bs, ct: (b * tpr + i, 0, 0)),
        pl.BlockSpec(memory_space=pl.ANY),
    ]
    args = [xs, mod, info, ysorted]
    if final_norm:
        in_specs.append(pl.BlockSpec((1, d), lambda b, i, bs, ct: (0, 0)))
        args.append(final_g.reshape(1, d))
    grid_spec = pltpu.PrefetchScalarGridSpec(
        num_scalar_prefetch=2,
        grid=(n_rows, tpr),
        in_specs=in_specs,
        out_specs=pl.BlockSpec((None, tm, d), lambda b, i, bs, ct: (b, i, 0)),
        scratch_shapes=[
            pltpu.VMEM((2, _moe_block_rows(tm), d), BF16),
            pltpu.SemaphoreType.DMA((2, N_EXPERTS)),
        ],
    )
    return pl.pallas_call(
        functools.partial(_moe_combine_kernel, tiles_per_row=tpr, n_tiles=n_rows * tpr, cap_rows=cap_rows,
                          final_norm=final_norm),
        grid_spec=grid_spec,
        out_shape=jax.ShapeDtypeStruct((n_rows if final_norm else nb, l, d), F32),
        input_output_aliases={} if final_norm else {2: 0},
        compiler_params=_cparams(("arbitrary", "arbitrary")),
        name="moe_combine",
    )(base, cnt, *args)


def _moe_layer(xs, mod, g, router, w1, w3, w2, *, n_rows, final_g=None):
    nb, l, d = xs.shape
    tm = min(TOKEN_TILE, l)
    row_tile = MOE_ROW_TILE
    n_tok = n_rows * l
    n_tiles = n_tok // tm
    cap_rows = _round_up(n_tok + BF16_SUBLANES * n_tiles, row_tile)
    xsorted, info, base, cnt, used = _moe_route(xs, mod, g, router, n_rows=n_rows, cap_rows=cap_rows,
                                                row_tile=row_tile)
    per_e = (used + row_tile - 1) // row_tile
    ends = jnp.cumsum(per_e)
    total = ends[-1]
    max_tiles = (2 * n_tok + (BF16_SUBLANES - 1) * N_EXPERTS * n_tiles) // row_tile + N_EXPERTS
    gidx = jnp.minimum(jnp.arange(max_tiles, dtype=jnp.int32), total - 1)
    te = jnp.sum((ends[None, :] <= gidx[:, None]).astype(jnp.int32), axis=1)
    tr = te * (cap_rows // row_tile) + gidx - (ends[te] - per_e[te])
    ta = (jnp.arange(max_tiles, dtype=jnp.int32) < total).astype(jnp.int32)
    ysorted = _moe_experts(xsorted, te, tr.astype(jnp.int32), ta, w1, w3, w2, row_tile=row_tile)
    return _moe_combine(xs, mod, info, base, cnt, ysorted, n_rows=n_rows, cap_rows=cap_rows, final_g=final_g)


def kernel(x, c, ctx, c_ctx, ada_w, ada_b, norm1_g, norm2_g, attn_w_qkv, attn_lq1, attn_lk1, attn_lq2, attn_lk2, attn_subln_g, attn_w_o, fnet_w_o, fnet_b_o, hy_w_in, hy_conv_w, hy_conv_b, hy_f_w1, hy_f_b1, hy_f_w2, hy_f_b2, hy_f_w3, hy_f_bias, hy_w_o, ff_w1, ff_w3, ff_w2, moe_router, moe_w1, moe_w3, moe_w2, final_g):
    batch, l, d = x.shape
    ctx_len = ctx.shape[1]
    assert d == DA_HEADS * 2 * DA_HEAD_DIM and l % ctx_len == 0 and (batch * ctx_len) % l == 0
    n_ctx_rows = batch * ctx_len // l
    nb = batch + n_ctx_rows
    xs = jnp.concatenate([x, ctx.reshape(n_ctx_rows, l, d)], axis=0)

    pad = _round_up(batch + 1, 8) - (batch + 1)
    c_all = jnp.concatenate([c, c_ctx[None, :], jnp.zeros((pad, d), F32)], axis=0)
    mod_all = _modulation(c_all, ada_w, ada_b)
    rope = _rope_tables(l)

    for i in range(DEPTH):
        last = i == DEPTH - 1
        kind, j = i % N_MIXERS, i // N_MIXERS
        n_rows = batch if last else nb
        m = mod_all[i]
        mod = jnp.concatenate([m[:batch], jnp.broadcast_to(m[batch:batch + 1], (n_ctx_rows, 6 * d))], axis=0)
        mod = mod.reshape(nb, 6, d)

        if kind == 0:
            lam_init = 0.8 - 0.6 * math.exp(-0.3 * i)
            wqkv = attn_w_qkv[j].astype(BF16)
            qk, vt = _norm_matmul(xs, mod, norm1_g[i], wqkv[:, :2 * d], n_rows=nb, rope=rope, n_lat=batch,
                                  w_t=wqkv[:, 2 * d:].T)
            lqk = jnp.stack([attn_lq1[j], attn_lk1[j], attn_lq2[j], attn_lk2[j]], axis=0)
            o = _attention(qk, vt, lqk, attn_subln_g[j], lam_init=lam_init, batch=batch, ctx_len=ctx_len,
                           with_ctx=not last)
            xs = _proj_residual(o, attn_w_o[j].astype(BF16), mod, xs, n_rows=n_rows, gate_row=2)
        elif kind == 1:
            xs = _fourier_layer(xs, mod, norm1_g[i], fnet_w_o[j], fnet_b_o[j], batch=batch, ctx_len=ctx_len,
                                with_ctx=not last)
        else:
            u = _norm_matmul(xs, mod, norm1_g[i], hy_w_in[j].astype(BF16), n_rows=n_rows)
            fp = (hy_f_w1[j], hy_f_b1[j], hy_f_w2[j], hy_f_b2[j], hy_f_w3[j])
            z = _hyena_conv(u, _hyena_spectra(l, d, *fp), hy_conv_w[j], hy_conv_b[j], hy_f_bias[j], None,
                            seq=l, n_seq=batch, seq_off=0)
            if not last:
                z = _hyena_conv(u, _hyena_spectra(ctx_len, d, *fp), hy_conv_w[j], hy_conv_b[j], hy_f_bias[j], z,
                                seq=ctx_len, n_seq=batch, seq_off=batch * (l // ctx_len))
            xs = _proj_residual(z, hy_w_o[j].astype(BF16), mod, xs, n_rows=n_rows, gate_row=2)

        k = i // 2
        if i % 2 == 0:
            xs = _ffn(xs, mod, norm2_g[i], ff_w1[k].astype(BF16), ff_w3[k].astype(BF16), ff_w2[k].astype(BF16),
                      n_rows=n_rows)
        else:
            xs = _moe_layer(xs, mod, norm2_g[i], moe_router[k], moe_w1[k].astype(BF16), moe_w3[k].astype(BF16),
                            moe_w2[k].astype(BF16), n_rows=n_rows, final_g=final_g if last else None)
    return xs
```
